```python
import jax, jax.numpy as jnp
from jax import lax
import numpy as np

D_MODEL = 1024
BATCH = 8
SEQ = 2048
DEPTH = 4

CONF_WIDTH = 512
CONF_KERNEL = 31
SC_WIDTH = 512
SC_KERNEL = 3
MLA_HEADS = 8
QK_NOPE = 64
QK_ROPE = 32
V_HEAD = 64
Q_LORA = 384
KV_LORA = 256
MLA_WIDTH = MLA_HEADS * V_HEAD
ROPE_THETA = 10000.0
Q_BLOCK = 128
N_BRANCH = 3
LN_EPS = 1e-5
RMS_EPS = 1e-6
DEEPNORM_ALPHA = (2 * DEPTH) ** 0.25
DEEPNORM_BETA = (8 * DEPTH) ** -0.25
ADA_SCALE = 0.5

IN_SIZES = (2 * CONF_WIDTH, CONF_WIDTH, 3 * SC_WIDTH, SC_WIDTH, Q_LORA, KV_LORA, QK_ROPE, MLA_WIDTH, N_BRANCH * D_MODEL)
D_IN = 2 * CONF_WIDTH + CONF_WIDTH + 3 * SC_WIDTH + SC_WIDTH + Q_LORA + KV_LORA + QK_ROPE + MLA_WIDTH + N_BRANCH * D_MODEL

kernel_name = "hybrid_conformer_shortconv_mla_deepnorm_adaln"


def layer_norm(x, g, b):
    x32 = x.astype(jnp.float32)
    mu = jnp.mean(x32, axis=-1, keepdims=True)
    var = jnp.mean(jnp.square(x32 - mu), axis=-1, keepdims=True)
    y = (x32 - mu) * lax.rsqrt(var + LN_EPS)
    return (y * g.astype(jnp.float32) + b.astype(jnp.float32)).astype(x.dtype)


def rms_norm(x, g):
    x32 = x.astype(jnp.float32)
    y = x32 * lax.rsqrt(jnp.mean(jnp.square(x32), axis=-1, keepdims=True) + RMS_EPS)
    return (y * g.astype(jnp.float32)).astype(x.dtype)


def causal_depthwise_conv(x, w):
    k_width, ch = w.shape
    return lax.conv_general_dilated(
        x, w.astype(x.dtype)[:, None, :], window_strides=(1,), padding=((k_width - 1, 0),),
        dimension_numbers=("NWC", "WIO", "NWC"), feature_group_count=ch)


def rope_tables(positions):
    inv_freq = ROPE_THETA ** (-jnp.arange(0, QK_ROPE, 2, dtype=jnp.float32) / QK_ROPE)
    ang = positions.astype(jnp.float32)[..., None] * inv_freq
    return jnp.cos(ang), jnp.sin(ang)


def apply_rope(x, cos, sin):
    cos = cos.astype(x.dtype)
    sin = sin.astype(x.dtype)
    x1, x2 = jnp.split(x, 2, axis=-1)
    return jnp.concatenate([x1 * cos - x2 * sin, x2 * cos + x1 * sin], axis=-1)


def causal_block_attention(q, k, v):
    b, s, h, dv = v.shape
    scale = (QK_NOPE + QK_ROPE) ** -0.5
    key_idx = jnp.arange(s)

    def one_block(i):
        start = i * Q_BLOCK
        qb = lax.dynamic_slice_in_dim(q, start, Q_BLOCK, axis=1)
        sc = jnp.einsum("bqhd,bkhd->bhqk", qb, k, preferred_element_type=jnp.float32) * scale
        q_idx = start + jnp.arange(Q_BLOCK)
        mask = key_idx[None, :] <= q_idx[:, None]
        sc = jnp.where(mask[None, None], sc, -jnp.inf)
        p = jax.nn.softmax(sc, axis=-1).astype(v.dtype)
        return jnp.einsum("bhqk,bkhd->bqhd", p, v)

    out = lax.map(one_block, jnp.arange(s // Q_BLOCK))
    return out.transpose(1, 0, 2, 3, 4).reshape(b, s, h, dv)


def hybrid_layer(x, c_act, cos, sin, w_ada, b_ada, w_in, conv_a_w, conv_a_b, ln_a_g, ln_a_b, w_a_out,
                 conv_b_w, w_b_out, q_norm_g, kv_norm_g, w_uq, w_ukv, w_c_out, w_o, ln_g, ln_b):
    b, s, _ = x.shape
    ada = c_act @ w_ada + b_ada
    shift, scale, gate = jnp.split(ada, 3, axis=-1)
    u = x * (1.0 + scale[:, None, :]) + shift[:, None, :]

    proj = jnp.einsum("bsd,dp->bsp", u, w_in)
    split_pts = [int(v) for v in np.cumsum(IN_SIZES)[:-1]]
    a_in, a_gate, b_in, b_gate, q_lat, kv_lat, k_rope, c_gate, merge_logits = jnp.split(proj, split_pts, axis=-1)

    a1, a2 = jnp.split(a_in, 2, axis=-1)
    a = a1 * jax.nn.sigmoid(a2)
    a = causal_depthwise_conv(a, conv_a_w) + conv_a_b
    a = jax.nn.silu(layer_norm(a, ln_a_g, ln_a_b))
    y_a = jnp.einsum("bsc,cd->bsd", a * jax.nn.silu(a_gate), w_a_out)

    xb, gb, gc = jnp.split(b_in, 3, axis=-1)
    yb = gb * causal_depthwise_conv(gc * xb, conv_b_w)
    y_b = jnp.einsum("bsc,cd->bsd", yb * jax.nn.silu(b_gate), w_b_out)

    q = jnp.einsum("bsr,rk->bsk", rms_norm(q_lat, q_norm_g), w_uq).reshape(b, s, MLA_HEADS, QK_NOPE + QK_ROPE)
    q_nope, q_pe = jnp.split(q, [QK_NOPE], axis=-1)
    q_pe = apply_rope(q_pe, cos[:, :, None, :], sin[:, :, None, :])
    kv = jnp.einsum("bsr,rk->bsk", rms_norm(kv_lat, kv_norm_g), w_ukv).reshape(b, s, MLA_HEADS, QK_NOPE + V_HEAD)
    k_nope, v = jnp.split(kv, [QK_NOPE], axis=-1)
    k_pe = apply_rope(k_rope, cos, sin)[:, :, None, :]
    qf = jnp.concatenate([q_nope, q_pe], axis=-1)
    kf = jnp.concatenate([k_nope, jnp.broadcast_to(k_pe, (b, s, MLA_HEADS, QK_ROPE))], axis=-1)
    o = causal_block_attention(qf, kf, v).reshape(b, s, MLA_WIDTH)
    y_c = jnp.einsum("bsc,cd->bsd", o * jax.nn.silu(c_gate), w_c_out)

    g_a, g_b, g_c = jnp.split(jax.nn.sigmoid(merge_logits), N_BRANCH, axis=-1)
    m = g_a * y_a + g_b * y_b + g_c * y_c
    out = jnp.einsum("bsd,de->bse", m, w_o)

    return layer_norm(DEEPNORM_ALPHA * x + gate[:, None, :] * out, ln_g, ln_b)


def setup_inputs(seed: int = 0) -> dict:
    key = jax.random.key(seed)
    ks = jax.random.split(key, 24)
    L, D = DEPTH, D_MODEL

    def nrm(k, shape, scale):
        return jax.random.normal(k, shape, dtype=jnp.float32) * scale

    x = nrm(ks[0], (BATCH, SEQ, D), 1.0)
    c = nrm(ks[1], (BATCH, D), 1.0)
    offsets = jax.random.randint(ks[2], (BATCH, 1), 0, 1024, dtype=jnp.int32)
    positions = (jnp.arange(SEQ, dtype=jnp.int32)[None, :] + offsets).astype(jnp.int32)
    return {
        "x": x,
        "c": c,
        "positions": positions,
        "w_ada": nrm(ks[3], (L, D, 3 * D), ADA_SCALE * D ** -0.5),
        "b_ada": nrm(ks[4], (L, 3 * D), 0.02),
        "w_in": nrm(ks[5], (L, D, D_IN), D ** -0.5),
        "conv_a_w": nrm(ks[6], (L, CONF_KERNEL, CONF_WIDTH), CONF_KERNEL ** -0.5),
        "conv_a_b": nrm(ks[7], (L, CONF_WIDTH), 0.02),
        "ln_a_g": 1.0 + nrm(ks[8], (L, CONF_WIDTH), 0.02),
        "ln_a_b": nrm(ks[9], (L, CONF_WIDTH), 0.02),
        "w_a_out": nrm(ks[10], (L, CONF_WIDTH, D), DEEPNORM_BETA * CONF_WIDTH ** -0.5),
        "conv_b_w": nrm(ks[11], (L, SC_KERNEL, SC_WIDTH), SC_KERNEL ** -0.5),
        "w_b_out": nrm(ks[12], (L, SC_WIDTH, D), DEEPNORM_BETA * SC_WIDTH ** -0.5),
        "q_norm_g": 1.0 + nrm(ks[13], (L, Q_LORA), 0.02),
        "kv_norm_g": 1.0 + nrm(ks[14], (L, KV_LORA), 0.02),
        "w_uq": nrm(ks[15], (L, Q_LORA, MLA_HEADS * (QK_NOPE + QK_ROPE)), Q_LORA ** -0.5),
        "w_ukv": nrm(ks[16], (L, KV_LORA, MLA_HEADS * (QK_NOPE + V_HEAD)), KV_LORA ** -0.5),
        "w_c_out": nrm(ks[17], (L, MLA_WIDTH, D), DEEPNORM_BETA * MLA_WIDTH ** -0.5),
        "w_o": nrm(ks[18], (L, D, D), DEEPNORM_BETA * D ** -0.5),
        "ln_g": 1.0 + nrm(ks[19], (L, D), 0.02),
        "ln_b": nrm(ks[20], (L, D), 0.02),
    }


def reference(x, c, positions, w_ada, b_ada, w_in, conv_a_w, conv_a_b, ln_a_g, ln_a_b, w_a_out,
              conv_b_w, w_b_out, q_norm_g, kv_norm_g, w_uq, w_ukv, w_c_out, w_o, ln_g, ln_b):
    c_act = jax.nn.silu(c)
    cos, sin = rope_tables(positions)
    h = x
    for l in range(DEPTH):
        h = hybrid_layer(h, c_act, cos, sin, w_ada[l], b_ada[l], w_in[l], conv_a_w[l], conv_a_b[l],
                         ln_a_g[l], ln_a_b[l], w_a_out[l], conv_b_w[l], w_b_out[l], q_norm_g[l],
                         kv_norm_g[l], w_uq[l], w_ukv[l], w_c_out[l], w_o[l], ln_g[l], ln_b[l])
    return h
```

```python
import functools

import jax
import jax.numpy as jnp
import numpy as np
from jax import lax
from jax.experimental import pallas as pl
from jax.experimental.pallas import tpu as pltpu

D_MODEL = 1024
BATCH = 8
SEQ = 2048
DEPTH = 4
CONF_WIDTH = 512
CONF_KERNEL = 31
SC_WIDTH = 512
SC_KERNEL = 3
MLA_HEADS = 8
QK_NOPE = 64
QK_ROPE = 32
V_HEAD = 64
Q_LORA = 384
KV_LORA = 256
MLA_WIDTH = MLA_HEADS * V_HEAD
ROPE_THETA = 10000.0
N_BRANCH = 3
LN_EPS = 1e-5
RMS_EPS = 1e-6
DEEPNORM_ALPHA = (2 * DEPTH) ** 0.25

LANES = 128
HEAD_PAD = LANES
HALF_ROPE = QK_ROPE // 2
QK_PAD = MLA_HEADS * HEAD_PAD
A_COLS = 3 * CONF_WIDTH
B_COLS = 4 * SC_WIDTH
C_COLS = Q_LORA + KV_LORA + HEAD_PAD
BACK_COLS = MLA_WIDTH + N_BRANCH * D_MODEL
KV_UP_COLS = QK_PAD + MLA_WIDTH

ROW_TILE = 256
CONV_CHUNK = 32
A_HALO = 32
B_HALO = 8
ATT_TILE = 256
VMEM_LIMIT = 48 * 1024 * 1024

_BF = jnp.bfloat16
_F32 = jnp.float32


def _dot(a, b):
    return jnp.dot(a, b, preferred_element_type=_F32)


def _sigmoid(x):
    return jax.nn.sigmoid(x)


def _silu(x):
    return x * jax.nn.sigmoid(x)


def _const_spec(shape):
    n = len(shape)
    return pl.BlockSpec(shape, lambda b, s: (0,) * n)


def _ada_kernel(c_ref, w_ref, b_ref, o_ref):
    c = c_ref[...]
    c_act = c * jax.nn.sigmoid(c)
    o_ref[...] = _dot(c_act, w_ref[...]) + b_ref[...]


def _ada_all_layers(c, w_ada, b_ada):
    b_r = b_ada.reshape(DEPTH, 3, 1, D_MODEL)
    return pl.pallas_call(
        _ada_kernel,
        grid=(DEPTH, 3),
        in_specs=[
            pl.BlockSpec((BATCH, D_MODEL), lambda l, j: (0, 0)),
            pl.BlockSpec((None, D_MODEL, D_MODEL), lambda l, j: (l, 0, j)),
            pl.BlockSpec((None, None, 1, D_MODEL), lambda l, j: (l, j, 0, 0)),
        ],
        out_specs=pl.BlockSpec((None, None, BATCH, D_MODEL), lambda l, j: (l, j, 0, 0)),
        out_shape=jax.ShapeDtypeStruct((DEPTH, 3, BATCH, D_MODEL), _F32),
        name="ada",
    )(c, w_ada, b_r)


def _angle_kernel(pos_ref, invf_ref, cos_ref, sin_ref):
    ang = pos_ref[...].astype(_F32) * invf_ref[...]
    cos_ref[...] = jnp.cos(ang)
    sin_ref[...] = jnp.sin(ang)


def _expand_kernel(cos_ref, sin_ref, ec_ref, es_ref, base_ref, c_out, s_out):
    hi = lax.Precision.HIGHEST
    c_out[...] = jnp.dot(cos_ref[...], ec_ref[...], precision=hi, preferred_element_type=_F32) + base_ref[...]
    s_out[...] = jnp.dot(sin_ref[...], es_ref[...], precision=hi, preferred_element_type=_F32)


def _rope_tables(positions):
    n_tok = BATCH * SEQ
    rows = n_tok * HALF_ROPE // LANES
    inv_freq = ROPE_THETA ** (-jnp.arange(0, QK_ROPE, 2, dtype=_F32) / QK_ROPE)
    pos_rep = jnp.repeat(positions.reshape(-1), HALF_ROPE).reshape(rows, LANES)
    invf_rep = jnp.tile(inv_freq, LANES // HALF_ROPE).reshape(1, LANES)
    cos_d, sin_d = pl.pallas_call(
        _angle_kernel,
        out_shape=[jax.ShapeDtypeStruct((rows, LANES), _F32)] * 2,
        name="rope_angles",
    )(pos_rep, invf_rep)
    cos16 = cos_d.reshape(n_tok, HALF_ROPE)
    sin16 = sin_d.reshape(n_tok, HALF_ROPE)

    eye = np.eye(HALF_ROPE, dtype=np.float32)
    ec = np.zeros((HALF_ROPE, LANES), np.float32)
    es = np.zeros((HALF_ROPE, LANES), np.float32)
    ec[:, QK_NOPE:QK_NOPE + HALF_ROPE] = eye
    ec[:, QK_NOPE + HALF_ROPE:QK_NOPE + QK_ROPE] = eye
    es[:, QK_NOPE:QK_NOPE + HALF_ROPE] = -eye
    es[:, QK_NOPE + HALF_ROPE:QK_NOPE + QK_ROPE] = eye
    base = np.ones((1, LANES), np.float32)
    base[:, QK_NOPE:QK_NOPE + QK_ROPE] = 0.0

    blk = 2048
    return pl.pallas_call(
        _expand_kernel,
        grid=(n_tok // blk,),
        in_specs=[
            pl.BlockSpec((blk, HALF_ROPE), lambda i: (i, 0)),
            pl.BlockSpec((blk, HALF_ROPE), lambda i: (i, 0)),
            pl.BlockSpec((HALF_ROPE, LANES), lambda i: (0, 0)),
            pl.BlockSpec((HALF_ROPE, LANES), lambda i: (0, 0)),
            pl.BlockSpec((1, LANES), lambda i: (0, 0)),
        ],
        out_specs=[pl.BlockSpec((blk, LANES), lambda i: (i, 0))] * 2,
        out_shape=[jax.ShapeDtypeStruct((n_tok, LANES), _F32)] * 2,
        name="rope_expand",
    )(cos16, sin16, jnp.asarray(ec), jnp.asarray(es), jnp.asarray(base))


def _rope128(t, c_tab, s_tab, lo_half):
    swapped = jnp.where(lo_half, pltpu.roll(t, LANES - HALF_ROPE, 1), pltpu.roll(t, HALF_ROPE, 1))
    return t * c_tab + swapped * s_tab


def _front_kernel(x_ref, shift_ref, scale_ref, wa_ref, wb_ref, wc_ref,
                  cw_ref, cb_ref, lag_ref, lab_ref, cbw_ref, qg_ref, kvg_ref,
                  wuq_ref, wukv_ref, ctab_ref, stab_ref,
                  za_ref, zb_ref, q_ref, k_ref, v_ref,
                  pa_ref, pb_ref, abuf_ref, bbuf_ref):
    tm = ROW_TILE

    @pl.when(pl.program_id(1) == 0)
    def _():
        abuf_ref[0:A_HALO, :] = jnp.zeros((A_HALO, CONF_WIDTH), _F32)
        bbuf_ref[0:B_HALO, :] = jnp.zeros((B_HALO, SC_WIDTH), _F32)

    u = (x_ref[...] * (1.0 + scale_ref[...]) + shift_ref[...]).astype(_BF)

    pa_ref[...] = _dot(u, wa_ref[...])
    abuf_ref[A_HALO:A_HALO + tm, :] = (
        pa_ref[:, 0:CONF_WIDTH] * _sigmoid(pa_ref[:, CONF_WIDTH:2 * CONF_WIDTH]))
    first = A_HALO - (CONF_KERNEL - 1)
    for r in range(0, tm, CONV_CHUNK):
        acc = jnp.broadcast_to(cb_ref[...], (CONV_CHUNK, CONF_WIDTH))
        for k in range(CONF_KERNEL):
            acc = acc + cw_ref[k:k + 1, :] * abuf_ref[first + k + r:first + k + r + CONV_CHUNK, :]
        mu = jnp.mean(acc, axis=-1, keepdims=True)
        cen = acc - mu
        var = jnp.mean(cen * cen, axis=-1, keepdims=True)
        a_n = cen * lax.rsqrt(var + LN_EPS) * lag_ref[...] + lab_ref[...]
        gate = pa_ref[r:r + CONV_CHUNK, 2 * CONF_WIDTH:3 * CONF_WIDTH]
        za_ref[r:r + CONV_CHUNK, :] = (_silu(a_n) * _silu(gate)).astype(_BF)
    abuf_ref[0:A_HALO, :] = abuf_ref[tm:tm + A_HALO, :]

    pb_ref[...] = _dot(u, wb_ref[...])
    bbuf_ref[B_HALO:B_HALO + tm, :] = pb_ref[:, 0:SC_WIDTH] * pb_ref[:, 2 * SC_WIDTH:3 * SC_WIDTH]
    firstb = B_HALO - (SC_KERNEL - 1)
    for r in range(0, tm, CONV_CHUNK):
        acc = cbw_ref[0:1, :] * bbuf_ref[firstb + r:firstb + r + CONV_CHUNK, :]
        for k in range(1, SC_KERNEL):
            acc = acc + cbw_ref[k:k + 1, :] * bbuf_ref[firstb + k + r:firstb + k + r + CONV_CHUNK, :]
        yb = pb_ref[r:r + CONV_CHUNK, SC_WIDTH:2 * SC_WIDTH] * acc
        zb_ref[r:r + CONV_CHUNK, :] = (yb * _silu(pb_ref[r:r + CONV_CHUNK, 3 * SC_WIDTH:4 * SC_WIDTH])).astype(_BF)
    bbuf_ref[0:B_HALO, :] = bbuf_ref[tm:tm + B_HALO, :]

    pc = _dot(u, wc_ref[...])
    q_lat = pc[:, 0:Q_LORA]
    kv_lat = pc[:, Q_LORA:Q_LORA + KV_LORA]
    k_rope = pc[:, Q_LORA + KV_LORA:C_COLS]
    qn = q_lat * lax.rsqrt(jnp.mean(q_lat * q_lat, axis=-1, keepdims=True) + RMS_EPS) * qg_ref[...]
    kvn = kv_lat * lax.rsqrt(jnp.mean(kv_lat * kv_lat, axis=-1, keepdims=True) + RMS_EPS) * kvg_ref[...]
    q_up = _dot(qn.astype(_BF), wuq_ref[...])
    kv_up = _dot(kvn.astype(_BF), wukv_ref[...])

    c_tab = ctab_ref[...]
    s_tab = stab_ref[...]
    lane = lax.broadcasted_iota(jnp.int32, (tm, LANES), 1)
    lo_half = lane < QK_NOPE + HALF_ROPE
    sm_scale = (QK_NOPE + QK_ROPE) ** -0.5
    k_pe = _rope128(k_rope, c_tab, s_tab, lo_half)
    for h in range(MLA_HEADS):
        sl = slice(h * HEAD_PAD, (h + 1) * HEAD_PAD)
        q_ref[:, sl] = (_rope128(q_up[:, sl], c_tab, s_tab, lo_half) * sm_scale).astype(_BF)
        k_ref[:, sl] = (kv_up[:, sl] + k_pe).astype(_BF)
    v_ref[...] = kv_up[:, QK_PAD:KV_UP_COLS].astype(_BF)


def _front(x, shift, scale, lw, c_tab, s_tab):
    tm = ROW_TILE
    n_s = SEQ // tm
    row = lambda b, s: (b * n_s + s, 0)
    per_b = lambda b, s: (b, 0, 0)
    in_specs = [
        pl.BlockSpec((tm, D_MODEL), row),
        pl.BlockSpec((None, 1, D_MODEL), per_b),
        pl.BlockSpec((None, 1, D_MODEL), per_b),
        _const_spec((D_MODEL, A_COLS)),
        _const_spec((D_MODEL, B_COLS)),
        _const_spec((D_MODEL, C_COLS)),
        _const_spec((CONF_KERNEL, CONF_WIDTH)),
        _const_spec((1, CONF_WIDTH)),
        _const_spec((1, CONF_WIDTH)),
        _const_spec((1, CONF_WIDTH)),
        _const_spec((SC_KERNEL, SC_WIDTH)),
        _const_spec((1, Q_LORA)),
        _const_spec((1, KV_LORA)),
        _const_spec((Q_LORA, QK_PAD)),
        _const_spec((KV_LORA, KV_UP_COLS)),
        pl.BlockSpec((tm, LANES), row),
        pl.BlockSpec((tm, LANES), row),
    ]
    n_tok = BATCH * SEQ
    out_shape = [
        jax.ShapeDtypeStruct((n_tok, CONF_WIDTH), _BF),
        jax.ShapeDtypeStruct((n_tok, SC_WIDTH), _BF),
        jax.ShapeDtypeStruct((n_tok, QK_PAD), _BF),
        jax.ShapeDtypeStruct((n_tok, QK_PAD), _BF),
        jax.ShapeDtypeStruct((n_tok, MLA_WIDTH), _BF),
    ]
    out_specs = [
        pl.BlockSpec((tm, CONF_WIDTH), row),
        pl.BlockSpec((tm, SC_WIDTH), row),
        pl.BlockSpec((tm, QK_PAD), row),
        pl.BlockSpec((tm, QK_PAD), row),
        pl.BlockSpec((tm, MLA_WIDTH), row),
    ]
    return pl.pallas_call(
        _front_kernel,
        grid=(BATCH, n_s),
        in_specs=in_specs,
        out_specs=out_specs,
        out_shape=out_shape,
        scratch_shapes=[
            pltpu.VMEM((tm, A_COLS), _F32),
            pltpu.VMEM((tm, B_COLS), _F32),
            pltpu.VMEM((tm + A_HALO, CONF_WIDTH), _F32),
            pltpu.VMEM((tm + B_HALO, SC_WIDTH), _F32),
        ],
        compiler_params=pltpu.CompilerParams(
            dimension_semantics=("arbitrary", "arbitrary"), vmem_limit_bytes=VMEM_LIMIT),
        name="front",
    )(x, shift, scale, lw["wa"], lw["wb"], lw["wc"], lw["conv_a_w"], lw["conv_a_b"],
      lw["ln_a_g"], lw["ln_a_b"], lw["conv_b_w"], lw["q_norm_g"], lw["kv_norm_g"],
      lw["wuq"], lw["wukv"], c_tab, s_tab)


def _attn_kernel(q_ref, k_ref, v_ref, o_ref):
    t = ATT_TILE
    qi = pl.program_id(2)
    row = lax.broadcasted_iota(jnp.int32, (t, t), 0)
    col = lax.broadcasted_iota(jnp.int32, (t, t), 1)
    causal = col <= row
    nt = (((1,), (1,)), ((), ()))
    outs = []
    for hh in range(2):
        hs = slice(hh * HEAD_PAD, (hh + 1) * HEAD_PAD)
        q = q_ref[:, hs]

        def step(start, carry, masked):
            m, l, acc = carry
            kb = k_ref[pl.ds(start, t), hs]
            s = lax.dot_general(q, kb, nt, preferred_element_type=_F32)
            if masked:
                s = jnp.where(causal, s, -jnp.inf)
            m_new = jnp.maximum(m, jnp.max(s, axis=-1, keepdims=True))
            alpha = jnp.exp(m - m_new)
            p = jnp.exp(s - m_new)
            l_new = alpha * l + jnp.sum(p, axis=-1, keepdims=True)
            pv = _dot(p.astype(_BF), v_ref[pl.ds(start, t), :])
            return m_new, l_new, alpha * acc + pv

        def body(j, carry):
            return step(pl.multiple_of(j * t, t), carry, False)

        init = (jnp.full((t, 1), -jnp.inf, _F32), jnp.zeros((t, 1), _F32),
                jnp.zeros((t, 2 * V_HEAD), _F32))
        carry = lax.fori_loop(0, qi, body, init)
        m, l, acc = step(pl.multiple_of(qi * t, t), carry, True)
        outs.append(acc / l)
    lane = lax.broadcasted_iota(jnp.int32, (t, 2 * V_HEAD), 1)
    o_ref[...] = jnp.where(lane < V_HEAD, outs[0], outs[1]).astype(o_ref.dtype)


def _attention(q, k, v):
    t = ATT_TILE
    n_q = SEQ // t
    n_pair = MLA_HEADS // 2
    q3 = q.reshape(BATCH, SEQ, QK_PAD)
    k3 = k.reshape(BATCH, SEQ, QK_PAD)
    v3 = v.reshape(BATCH, SEQ, MLA_WIDTH)
    o = pl.pallas_call(
        _attn_kernel,
        grid=(BATCH, n_pair, n_q),
        in_specs=[
            pl.BlockSpec((None, t, 2 * HEAD_PAD), lambda b, g, i: (b, i, g)),
            pl.BlockSpec((None, SEQ, 2 * HEAD_PAD), lambda b, g, i: (b, 0, g)),
            pl.BlockSpec((None, SEQ, 2 * V_HEAD), lambda b, g, i: (b, 0, g)),
        ],
        out_specs=pl.BlockSpec((None, t, 2 * V_HEAD), lambda b, g, i: (b, i, g)),
        out_shape=jax.ShapeDtypeStruct((BATCH, SEQ, MLA_WIDTH), _BF),
        compiler_params=pltpu.CompilerParams(
            dimension_semantics=("arbitrary", "arbitrary", "arbitrary"), vmem_limit_bytes=VMEM_LIMIT),
        name="attn",
    )(q3, k3, v3)
    return o.reshape(BATCH * SEQ, MLA_WIDTH)


def _back_kernel(x_ref, shift_ref, scale_ref, gate_ref, za_ref, zb_ref, o_ref,
                 w3_ref, wao_ref, wbo_ref, wco_ref, wo_ref, lg_ref, lb_ref, out_ref):
    x = x_ref[...]
    u = (x * (1.0 + scale_ref[...]) + shift_ref[...]).astype(_BF)
    p3 = _dot(u, w3_ref[...])
    c_gate = p3[:, 0:MLA_WIDTH]
    oc = (o_ref[...].astype(_F32) * _silu(c_gate)).astype(_BF)
    m = _sigmoid(p3[:, MLA_WIDTH:MLA_WIDTH + D_MODEL]) * _dot(za_ref[...], wao_ref[...])
    m = m + _sigmoid(p3[:, MLA_WIDTH + D_MODEL:MLA_WIDTH + 2 * D_MODEL]) * _dot(zb_ref[...], wbo_ref[...])
    m = m + _sigmoid(p3[:, MLA_WIDTH + 2 * D_MODEL:BACK_COLS]) * _dot(oc, wco_ref[...])
    out = _dot(m.astype(_BF), wo_ref[...])
    r = DEEPNORM_ALPHA * x + gate_ref[...] * out
    mu = jnp.mean(r, axis=-1, keepdims=True)
    cen = r - mu
    var = jnp.mean(cen * cen, axis=-1, keepdims=True)
    out_ref[...] = cen * lax.rsqrt(var + LN_EPS) * lg_ref[...] + lb_ref[...]


def _back(x, shift, scale, gate, za, zb, o, lw):
    tm = ROW_TILE
    n_s = SEQ // tm
    row = lambda b, s: (b * n_s + s, 0)
    per_b = lambda b, s: (b, 0, 0)
    in_specs = [
        pl.BlockSpec((tm, D_MODEL), row),
        pl.BlockSpec((None, 1, D_MODEL), per_b),
        pl.BlockSpec((None, 1, D_MODEL), per_b),
        pl.BlockSpec((None, 1, D_MODEL), per_b),
        pl.BlockSpec((tm, CONF_WIDTH), row),
        pl.BlockSpec((tm, SC_WIDTH), row),
        pl.BlockSpec((tm, MLA_WIDTH), row),
        _const_spec((D_MODEL, BACK_COLS)),
        _const_spec((CONF_WIDTH, D_MODEL)),
        _const_spec((SC_WIDTH, D_MODEL)),
        _const_spec((MLA_WIDTH, D_MODEL)),
        _const_spec((D_MODEL, D_MODEL)),
        _const_spec((1, D_MODEL)),
        _const_spec((1, D_MODEL)),
    ]
    return pl.pallas_call(
        _back_kernel,
        grid=(BATCH, n_s),
        in_specs=in_specs,
        out_specs=pl.BlockSpec((tm, D_MODEL), row),
        out_shape=jax.ShapeDtypeStruct((BATCH * SEQ, D_MODEL), _F32),
        compiler_params=pltpu.CompilerParams(
            dimension_semantics=("arbitrary", "arbitrary"), vmem_limit_bytes=VMEM_LIMIT),
        name="back",
    )(x, shift, scale, gate, za, zb, o, lw["w3"], lw["w_a_out"], lw["w_b_out"], lw["w_c_out"],
      lw["w_o"], lw["ln_g"], lw["ln_b"])


def _layer_weights(l, w_in, conv_a_w, conv_a_b, ln_a_g, ln_a_b, w_a_out, conv_b_w, w_b_out,
                   q_norm_g, kv_norm_g, w_uq, w_ukv, w_c_out, w_o, ln_g, ln_b):
    wi = w_in[l]
    o_a = 0
    o_b = A_COLS
    o_q = o_b + B_COLS
    o_kv = o_q + Q_LORA
    o_kr = o_kv + KV_LORA
    o_cg = o_kr + QK_ROPE
    zeros = lambda n: jnp.zeros((D_MODEL, n), w_in.dtype)
    wc = jnp.concatenate(
        [wi[:, o_q:o_kr], zeros(QK_NOPE), wi[:, o_kr:o_cg], zeros(HEAD_PAD - QK_NOPE - QK_ROPE)], axis=1)
    wuq = w_uq[l].reshape(Q_LORA, MLA_HEADS, QK_NOPE + QK_ROPE)
    wuq = jnp.pad(wuq, ((0, 0), (0, 0), (0, HEAD_PAD - QK_NOPE - QK_ROPE))).reshape(Q_LORA, QK_PAD)
    wukv = w_ukv[l].reshape(KV_LORA, MLA_HEADS, QK_NOPE + V_HEAD)
    wk = jnp.pad(wukv[:, :, :QK_NOPE], ((0, 0), (0, 0), (0, HEAD_PAD - QK_NOPE))).reshape(KV_LORA, QK_PAD)
    wv = wukv[:, :, QK_NOPE:].reshape(KV_LORA, MLA_WIDTH)
    row = lambda a: a[l].reshape(1, -1)
    return {
        "wa": wi[:, o_a:o_b].astype(_BF),
        "wb": wi[:, o_b:o_q].astype(_BF),
        "wc": wc.astype(_BF),
        "w3": wi[:, o_cg:].astype(_BF),
        "conv_a_w": conv_a_w[l], "conv_a_b": row(conv_a_b), "ln_a_g": row(ln_a_g), "ln_a_b": row(ln_a_b),
        "conv_b_w": conv_b_w[l], "q_norm_g": row(q_norm_g), "kv_norm_g": row(kv_norm_g),
        "wuq": wuq.astype(_BF),
        "wukv": jnp.concatenate([wk, wv], axis=1).astype(_BF),
        "w_a_out": w_a_out[l].astype(_BF), "w_b_out": w_b_out[l].astype(_BF),
        "w_c_out": w_c_out[l].astype(_BF), "w_o": w_o[l].astype(_BF),
        "ln_g": row(ln_g), "ln_b": row(ln_b),
    }


def kernel(x, c, positions, w_ada, b_ada, w_in, conv_a_w, conv_a_b, ln_a_g, ln_a_b, w_a_out, conv_b_w, w_b_out, q_norm_g, kv_norm_g, w_uq, w_ukv, w_c_out, w_o, ln_g, ln_b):
    assert x.shape == (BATCH, SEQ, D_MODEL) and x.dtype == _F32
    ada = _ada_all_layers(c, w_ada, b_ada)
    c_tab, s_tab = _rope_tables(positions)
    h = x.reshape(BATCH * SEQ, D_MODEL)
    for l in range(DEPTH):
        lw = _layer_weights(l, w_in, conv_a_w, conv_a_b, ln_a_g, ln_a_b, w_a_out, conv_b_w, w_b_out,
                            q_norm_g, kv_norm_g, w_uq, w_ukv, w_c_out, w_o, ln_g, ln_b)
        shift = ada[l, 0].reshape(BATCH, 1, D_MODEL)
        scale = ada[l, 1].reshape(BATCH, 1, D_MODEL)
        gate = ada[l, 2].reshape(BATCH, 1, D_MODEL)
        za, zb, q, k, v = _front(h, shift, scale, lw, c_tab, s_tab)
        o = _attention(q, k, v)
        h = _back(h, shift, scale, gate, za, zb, o, lw)
    return h.reshape(BATCH, SEQ, D_MODEL)
```

```python
import functools

import jax
import jax.numpy as jnp
import numpy as np
from jax import lax
from jax.experimental import pallas as pl
from jax.experimental.pallas import tpu as pltpu

D_MODEL = 1024
BATCH = 8
SEQ = 2048
DEPTH = 4
CONF_WIDTH = 512
CONF_KERNEL = 31
SC_WIDTH = 512
SC_KERNEL = 3
MLA_HEADS = 8
QK_NOPE = 64
QK_ROPE = 32
V_HEAD = 64
Q_LORA = 384
KV_LORA = 256
MLA_WIDTH = MLA_HEADS * V_HEAD
ROPE_THETA = 10000.0
N_BRANCH = 3
LN_EPS = 1e-5
RMS_EPS = 1e-6
DEEPNORM_ALPHA = (2 * DEPTH) ** 0.25

LANES = 128
HEAD_PAD = LANES
HALF_ROPE = QK_ROPE // 2
QK_PAD = MLA_HEADS * HEAD_PAD
A_COLS = 3 * CONF_WIDTH
B_COLS = 4 * SC_WIDTH
C_COLS = Q_LORA + KV_LORA + HEAD_PAD
BACK_COLS = MLA_WIDTH + N_BRANCH * D_MODEL
KV_UP_COLS = 2 * QK_PAD

ROW_TILE = 256
CONV_CHUNK = 32
A_HALO = 32
B_HALO = 8
ATT_TILE = 256
ATT_HEADS = 8
VMEM_LIMIT = 48 * 1024 * 1024

_BF = jnp.bfloat16
_F32 = jnp.float32


def _dot(a, b):
    return jnp.dot(a, b, preferred_element_type=_F32)


def _sigmoid(x):
    return jax.nn.sigmoid(x)


def _silu(x):
    return x * jax.nn.sigmoid(x)


def _const_spec(shape):
    n = len(shape)
    return pl.BlockSpec(shape, lambda b, s: (0,) * n)


def _ada_kernel(c_ref, w_ref, b_ref, o_ref):
    c = c_ref[...]
    c_act = c * jax.nn.sigmoid(c)
    o_ref[...] = _dot(c_act, w_ref[...]) + b_ref[...]


def _ada_all_layers(c, w_ada, b_ada):
    b_r = b_ada.reshape(DEPTH, 3, 1, D_MODEL)
    return pl.pallas_call(
        _ada_kernel,
        grid=(DEPTH, 3),
        in_specs=[
            pl.BlockSpec((BATCH, D_MODEL), lambda l, j: (0, 0)),
            pl.BlockSpec((None, D_MODEL, D_MODEL), lambda l, j: (l, 0, j)),
            pl.BlockSpec((None, None, 1, D_MODEL), lambda l, j: (l, j, 0, 0)),
        ],
        out_specs=pl.BlockSpec((None, None, BATCH, D_MODEL), lambda l, j: (l, j, 0, 0)),
        out_shape=jax.ShapeDtypeStruct((DEPTH, 3, BATCH, D_MODEL), _F32),
        name="ada",
    )(c, w_ada, b_r)


def _angle_kernel(pos_ref, invf_ref, cos_ref, sin_ref):
    ang = pos_ref[...].astype(_F32) * invf_ref[...]
    cos_ref[...] = jnp.cos(ang)
    sin_ref[...] = jnp.sin(ang)


def _expand_kernel(cos_ref, sin_ref, ec_ref, es_ref, base_ref, c_out, s_out):
    hi = lax.Precision.HIGHEST
    c_out[...] = jnp.dot(cos_ref[...], ec_ref[...], precision=hi, preferred_element_type=_F32) + base_ref[...]
    s_out[...] = jnp.dot(sin_ref[...], es_ref[...], precision=hi, preferred_element_type=_F32)


def _rope_tables(positions):
    n_tok = BATCH * SEQ
    rows = n_tok * HALF_ROPE // LANES
    inv_freq = ROPE_THETA ** (-jnp.arange(0, QK_ROPE, 2, dtype=_F32) / QK_ROPE)
    pos_rep = jnp.repeat(positions.reshape(-1), HALF_ROPE).reshape(rows, LANES)
    invf_rep = jnp.tile(inv_freq, LANES // HALF_ROPE).reshape(1, LANES)
    cos_d, sin_d = pl.pallas_call(
        _angle_kernel,
        out_shape=[jax.ShapeDtypeStruct((rows, LANES), _F32)] * 2,
        name="rope_angles",
    )(pos_rep, invf_rep)
    cos16 = cos_d.reshape(n_tok, HALF_ROPE)
    sin16 = sin_d.reshape(n_tok, HALF_ROPE)

    eye = np.eye(HALF_ROPE, dtype=np.float32)
    ec = np.zeros((HALF_ROPE, LANES), np.float32)
    es = np.zeros((HALF_ROPE, LANES), np.float32)
    ec[:, QK_NOPE:QK_NOPE + HALF_ROPE] = eye
    ec[:, QK_NOPE + HALF_ROPE:QK_NOPE + QK_ROPE] = eye
    es[:, QK_NOPE:QK_NOPE + HALF_ROPE] = -eye
    es[:, QK_NOPE + HALF_ROPE:QK_NOPE + QK_ROPE] = eye
    base = np.ones((1, LANES), np.float32)
    base[:, QK_NOPE:QK_NOPE + QK_ROPE] = 0.0

    blk = 2048
    return pl.pallas_call(
        _expand_kernel,
        grid=(n_tok // blk,),
        in_specs=[
            pl.BlockSpec((blk, HALF_ROPE), lambda i: (i, 0)),
            pl.BlockSpec((blk, HALF_ROPE), lambda i: (i, 0)),
            pl.BlockSpec((HALF_ROPE, LANES), lambda i: (0, 0)),
            pl.BlockSpec((HALF_ROPE, LANES), lambda i: (0, 0)),
            pl.BlockSpec((1, LANES), lambda i: (0, 0)),
        ],
        out_specs=[pl.BlockSpec((blk, LANES), lambda i: (i, 0))] * 2,
        out_shape=[jax.ShapeDtypeStruct((n_tok, LANES), _F32)] * 2,
        name="rope_expand",
    )(cos16, sin16, jnp.asarray(ec), jnp.asarray(es), jnp.asarray(base))


def _rope128(t, c_tab, s_tab, lo_half):
    swapped = jnp.where(lo_half, pltpu.roll(t, LANES - HALF_ROPE, 1), pltpu.roll(t, HALF_ROPE, 1))
    return t * c_tab + swapped * s_tab


def _front_kernel(x_ref, shift_ref, scale_ref, wa_ref, wb_ref, wc_ref,
                  cw_ref, cb_ref, lag_ref, lab_ref, cbw_ref, qg_ref, kvg_ref,
                  wuq_ref, wukv_ref, vones_ref, ctab_ref, stab_ref,
                  za_ref, zb_ref, q_ref, k_ref, v_ref,
                  pa_ref, pb_ref, abuf_ref, bbuf_ref):
    tm = ROW_TILE

    @pl.when(pl.program_id(1) == 0)
    def _():
        abuf_ref[0:A_HALO, :] = jnp.zeros((A_HALO, CONF_WIDTH), _F32)
        bbuf_ref[0:B_HALO, :] = jnp.zeros((B_HALO, SC_WIDTH), _F32)

    u = (x_ref[...] * (1.0 + scale_ref[...]) + shift_ref[...]).astype(_BF)

    pa_ref[...] = _dot(u, wa_ref[...])
    abuf_ref[A_HALO:A_HALO + tm, :] = (
        pa_ref[:, 0:CONF_WIDTH] * _sigmoid(pa_ref[:, CONF_WIDTH:2 * CONF_WIDTH]))
    first = A_HALO - (CONF_KERNEL - 1)
    for r in range(0, tm, CONV_CHUNK):
        acc = jnp.broadcast_to(cb_ref[...], (CONV_CHUNK, CONF_WIDTH))
        for k in range(CONF_KERNEL):
            acc = acc + cw_ref[k:k + 1, :] * abuf_ref[first + k + r:first + k + r + CONV_CHUNK, :]
        mu = jnp.mean(acc, axis=-1, keepdims=True)
        cen = acc - mu
        var = jnp.mean(cen * cen, axis=-1, keepdims=True)
        a_n = cen * lax.rsqrt(var + LN_EPS) * lag_ref[...] + lab_ref[...]
        gate = pa_ref[r:r + CONV_CHUNK, 2 * CONF_WIDTH:3 * CONF_WIDTH]
        za_ref[r:r + CONV_CHUNK, :] = (_silu(a_n) * _silu(gate)).astype(_BF)
    abuf_ref[0:A_HALO, :] = abuf_ref[tm:tm + A_HALO, :]

    pb_ref[...] = _dot(u, wb_ref[...])
    bbuf_ref[B_HALO:B_HALO + tm, :] = pb_ref[:, 0:SC_WIDTH] * pb_ref[:, 2 * SC_WIDTH:3 * SC_WIDTH]
    firstb = B_HALO - (SC_KERNEL - 1)
    for r in range(0, tm, CONV_CHUNK):
        acc = cbw_ref[0:1, :] * bbuf_ref[firstb + r:firstb + r + CONV_CHUNK, :]
        for k in range(1, SC_KERNEL):
            acc = acc + cbw_ref[k:k + 1, :] * bbuf_ref[firstb + k + r:firstb + k + r + CONV_CHUNK, :]
        yb = pb_ref[r:r + CONV_CHUNK, SC_WIDTH:2 * SC_WIDTH] * acc
        zb_ref[r:r + CONV_CHUNK, :] = (yb * _silu(pb_ref[r:r + CONV_CHUNK, 3 * SC_WIDTH:4 * SC_WIDTH])).astype(_BF)
    bbuf_ref[0:B_HALO, :] = bbuf_ref[tm:tm + B_HALO, :]

    pc = _dot(u, wc_ref[...])
    q_lat = pc[:, 0:Q_LORA]
    kv_lat = pc[:, Q_LORA:Q_LORA + KV_LORA]
    k_rope = pc[:, Q_LORA + KV_LORA:C_COLS]
    qn = q_lat * lax.rsqrt(jnp.mean(q_lat * q_lat, axis=-1, keepdims=True) + RMS_EPS) * qg_ref[...]
    kvn = kv_lat * lax.rsqrt(jnp.mean(kv_lat * kv_lat, axis=-1, keepdims=True) + RMS_EPS) * kvg_ref[...]
    q_up = _dot(qn.astype(_BF), wuq_ref[...])
    kv_up = _dot(kvn.astype(_BF), wukv_ref[...])

    c_tab = ctab_ref[...]
    s_tab = stab_ref[...]
    lane = lax.broadcasted_iota(jnp.int32, (tm, LANES), 1)
    lo_half = lane < QK_NOPE + HALF_ROPE
    sm_scale = (QK_NOPE + QK_ROPE) ** -0.5 * np.log2(np.e)
    k_pe = _rope128(k_rope, c_tab, s_tab, lo_half)
    for h in range(MLA_HEADS):
        sl = slice(h * HEAD_PAD, (h + 1) * HEAD_PAD)
        q_ref[:, sl] = (_rope128(q_up[:, sl], c_tab, s_tab, lo_half) * sm_scale).astype(_BF)
        k_ref[:, sl] = (kv_up[:, sl] + k_pe).astype(_BF)
    v_ref[...] = (kv_up[:, QK_PAD:KV_UP_COLS] + vones_ref[...]).astype(_BF)


def _v_ones_row():
    pair = np.concatenate([np.zeros(V_HEAD), np.ones(2 * V_HEAD), np.zeros(V_HEAD)]).astype(np.float32)
    return jnp.asarray(np.tile(pair, MLA_HEADS // 2).reshape(1, QK_PAD))


def _front(x, shift, scale, lw, c_tab, s_tab):
    tm = ROW_TILE
    n_s = SEQ // tm
    row = lambda b, s: (b * n_s + s, 0)
    per_b = lambda b, s: (b, 0, 0)
    in_specs = [
        pl.BlockSpec((tm, D_MODEL), row),
        pl.BlockSpec((None, 1, D_MODEL), per_b),
        pl.BlockSpec((None, 1, D_MODEL), per_b),
        _const_spec((D_MODEL, A_COLS)),
        _const_spec((D_MODEL, B_COLS)),
        _const_spec((D_MODEL, C_COLS)),
        _const_spec((CONF_KERNEL, CONF_WIDTH)),
        _const_spec((1, CONF_WIDTH)),
        _const_spec((1, CONF_WIDTH)),
        _const_spec((1, CONF_WIDTH)),
        _const_spec((SC_KERNEL, SC_WIDTH)),
        _const_spec((1, Q_LORA)),
        _const_spec((1, KV_LORA)),
        _const_spec((Q_LORA, QK_PAD)),
        _const_spec((KV_LORA, KV_UP_COLS)),
        _const_spec((1, QK_PAD)),
        pl.BlockSpec((tm, LANES), row),
        pl.BlockSpec((tm, LANES), row),
    ]
    n_tok = BATCH * SEQ
    out_shape = [
        jax.ShapeDtypeStruct((n_tok, CONF_WIDTH), _BF),
        jax.ShapeDtypeStruct((n_tok, SC_WIDTH), _BF),
        jax.ShapeDtypeStruct((n_tok, QK_PAD), _BF),
        jax.ShapeDtypeStruct((n_tok, QK_PAD), _BF),
        jax.ShapeDtypeStruct((n_tok, QK_PAD), _BF),
    ]
    out_specs = [
        pl.BlockSpec((tm, CONF_WIDTH), row),
        pl.BlockSpec((tm, SC_WIDTH), row),
        pl.BlockSpec((tm, QK_PAD), row),
        pl.BlockSpec((tm, QK_PAD), row),
        pl.BlockSpec((tm, QK_PAD), row),
    ]
    return pl.pallas_call(
        _front_kernel,
        grid=(BATCH, n_s),
        in_specs=in_specs,
        out_specs=out_specs,
        out_shape=out_shape,
        scratch_shapes=[
            pltpu.VMEM((tm, A_COLS), _F32),
            pltpu.VMEM((tm, B_COLS), _F32),
            pltpu.VMEM((tm + A_HALO, CONF_WIDTH), _F32),
            pltpu.VMEM((tm + B_HALO, SC_WIDTH), _F32),
        ],
        compiler_params=pltpu.CompilerParams(
            dimension_semantics=("arbitrary", "arbitrary"), vmem_limit_bytes=VMEM_LIMIT),
        name="front",
    )(x, shift, scale, lw["wa"], lw["wb"], lw["wc"], lw["conv_a_w"], lw["conv_a_b"],
      lw["ln_a_g"], lw["ln_a_b"], lw["conv_b_w"], lw["q_norm_g"], lw["kv_norm_g"],
      lw["wuq"], lw["wukv"], _v_ones_row(), c_tab, s_tab)


def _attn_kernel(q_ref, k_ref, v_ref, o_ref, s_buf, mx_ref, acc_ref):
    t = ATT_TILE
    qi = pl.program_id(2)
    nt = (((1,), (1,)), ((), ()))
    heads = range(ATT_HEADS)
    hs = [slice(h * HEAD_PAD, (h + 1) * HEAD_PAD) for h in heads]

    def scores(j, h):
        kb = k_ref[pl.ds(pl.multiple_of(j * t, t), t), hs[h]]
        return lax.dot_general(q_ref[:, hs[h]], kb, nt, preferred_element_type=_F32)

    def lane_max(s):
        out = s[:, 0:LANES]
        for c in range(1, t // LANES):
            out = jnp.maximum(out, s[:, c * LANES:(c + 1) * LANES])
        return out

    for h in heads:
        mx_ref[h] = jnp.full((t, LANES), -jnp.inf, _F32)
        acc_ref[h] = jnp.zeros((t, LANES), _F32)

    def pass1(j0, n_blocks):
        for h in heads:
            mx = mx_ref[h]
            for b in range(n_blocks):
                s = scores(j0 + b, h)
                s_buf[j0 + b, h] = s
                mx = jnp.maximum(mx, lane_max(s))
            mx_ref[h] = mx

    def pass1_pair(jj, carry):
        pass1(2 * jj, 2)
        return carry

    lax.fori_loop(0, qi // 2, pass1_pair, 0)

    @pl.when(qi % 2 == 1)
    def _():
        pass1(qi - 1, 1)

    row = lax.broadcasted_iota(jnp.int32, (t, t), 0)
    col = lax.broadcasted_iota(jnp.int32, (t, t), 1)
    causal = col <= row
    for h in heads:
        s = jnp.where(causal, scores(qi, h), -jnp.inf)
        s_buf[qi, h] = s
        mx_ref[h] = jnp.maximum(mx_ref[h], lane_max(s))

    m = [jnp.max(mx_ref[h], axis=-1, keepdims=True) for h in heads]

    def pass2(j0, n_blocks):
        vbs = [v_ref[pl.ds(pl.multiple_of((j0 + b) * t, t), t), :] for b in range(n_blocks)]
        for h in heads:
            acc = acc_ref[h]
            for b in range(n_blocks):
                p = jnp.exp2(s_buf[j0 + b, h] - m[h]).astype(_BF)
                acc = acc + _dot(p, vbs[b][:, hs[h]])
            acc_ref[h] = acc

    def pass2_pair(jj, carry):
        pass2(2 * jj, 2)
        return carry

    n_keys = qi + 1
    lax.fori_loop(0, n_keys // 2, pass2_pair, 0)

    @pl.when(n_keys % 2 == 1)
    def _():
        pass2(qi, 1)

    lane = lax.broadcasted_iota(jnp.int32, (t, LANES), 1)
    for g in range(ATT_HEADS // 2):
        even = acc_ref[2 * g]
        odd = acc_ref[2 * g + 1]
        den = jnp.where(lane < V_HEAD, pltpu.roll(even, V_HEAD, 1), pltpu.roll(odd, V_HEAD, 1))
        num = jnp.where(lane < V_HEAD, even, odd)
        o_ref[:, g * LANES:(g + 1) * LANES] = (num / den).astype(o_ref.dtype)


def _attention(q, k, v):
    t = ATT_TILE
    n_q = SEQ // t
    n_grp = MLA_HEADS // ATT_HEADS
    q3 = q.reshape(BATCH, SEQ, QK_PAD)
    k3 = k.reshape(BATCH, SEQ, QK_PAD)
    v3 = v.reshape(BATCH, SEQ, QK_PAD)
    o = pl.pallas_call(
        _attn_kernel,
        grid=(BATCH, n_grp, n_q),
        in_specs=[
            pl.BlockSpec((None, t, ATT_HEADS * HEAD_PAD), lambda b, g, i: (b, i, g)),
            pl.BlockSpec((None, SEQ, ATT_HEADS * HEAD_PAD), lambda b, g, i: (b, 0, g)),
            pl.BlockSpec((None, SEQ, ATT_HEADS * HEAD_PAD), lambda b, g, i: (b, 0, g)),
        ],
        out_specs=pl.BlockSpec((None, t, ATT_HEADS * V_HEAD), lambda b, g, i: (b, i, g)),
        out_shape=jax.ShapeDtypeStruct((BATCH, SEQ, MLA_WIDTH), _BF),
        scratch_shapes=[
            pltpu.VMEM((n_q, ATT_HEADS, t, t), _F32),
            pltpu.VMEM((ATT_HEADS, t, LANES), _F32),
            pltpu.VMEM((ATT_HEADS, t, LANES), _F32),
        ],
        compiler_params=pltpu.CompilerParams(
            dimension_semantics=("arbitrary", "arbitrary", "arbitrary"), vmem_limit_bytes=VMEM_LIMIT),
        name="attn",
    )(q3, k3, v3)
    return o.reshape(BATCH * SEQ, MLA_WIDTH)


def _back_kernel(x_ref, shift_ref, scale_ref, gate_ref, za_ref, zb_ref, o_ref,
                 w3_ref, wao_ref, wbo_ref, wco_ref, wo_ref, lg_ref, lb_ref, out_ref):
    x = x_ref[...]
    u = (x * (1.0 + scale_ref[...]) + shift_ref[...]).astype(_BF)
    p3 = _dot(u, w3_ref[...])
    c_gate = p3[:, 0:MLA_WIDTH]
    oc = (o_ref[...].astype(_F32) * _silu(c_gate)).astype(_BF)
    m = _sigmoid(p3[:, MLA_WIDTH:MLA_WIDTH + D_MODEL]) * _dot(za_ref[...], wao_ref[...])
    m = m + _sigmoid(p3[:, MLA_WIDTH + D_MODEL:MLA_WIDTH + 2 * D_MODEL]) * _dot(zb_ref[...], wbo_ref[...])
    m = m + _sigmoid(p3[:, MLA_WIDTH + 2 * D_MODEL:BACK_COLS]) * _dot(oc, wco_ref[...])
    out = _dot(m.astype(_BF), wo_ref[...])
    r = DEEPNORM_ALPHA * x + gate_ref[...] * out
    mu = jnp.mean(r, axis=-1, keepdims=True)
    cen = r - mu
    var = jnp.mean(cen * cen, axis=-1, keepdims=True)
    out_ref[...] = cen * lax.rsqrt(var + LN_EPS) * lg_ref[...] + lb_ref[...]


def _back(x, shift, scale, gate, za, zb, o, lw):
    tm = ROW_TILE
    n_s = SEQ // tm
    row = lambda b, s: (b * n_s + s, 0)
    per_b = lambda b, s: (b, 0, 0)
    in_specs = [
        pl.BlockSpec((tm, D_MODEL), row),
        pl.BlockSpec((None, 1, D_MODEL), per_b),
        pl.BlockSpec((None, 1, D_MODEL), per_b),
        pl.BlockSpec((None, 1, D_MODEL), per_b),
        pl.BlockSpec((tm, CONF_WIDTH), row),
        pl.BlockSpec((tm, SC_WIDTH), row),
        pl.BlockSpec((tm, MLA_WIDTH), row),
        _const_spec((D_MODEL, BACK_COLS)),
        _const_spec((CONF_WIDTH, D_MODEL)),
        _const_spec((SC_WIDTH, D_MODEL)),
        _const_spec((MLA_WIDTH, D_MODEL)),
        _const_spec((D_MODEL, D_MODEL)),
        _const_spec((1, D_MODEL)),
        _const_spec((1, D_MODEL)),
    ]
    return pl.pallas_call(
        _back_kernel,
        grid=(BATCH, n_s),
        in_specs=in_specs,
        out_specs=pl.BlockSpec((tm, D_MODEL), row),
        out_shape=jax.ShapeDtypeStruct((BATCH * SEQ, D_MODEL), _F32),
        compiler_params=pltpu.CompilerParams(
            dimension_semantics=("arbitrary", "arbitrary"), vmem_limit_bytes=VMEM_LIMIT),
        name="back",
    )(x, shift, scale, gate, za, zb, o, lw["w3"], lw["w_a_out"], lw["w_b_out"], lw["w_c_out"],
      lw["w_o"], lw["ln_g"], lw["ln_b"])


def _layer_weights(l, w_in, conv_a_w, conv_a_b, ln_a_g, ln_a_b, w_a_out, conv_b_w, w_b_out,
                   q_norm_g, kv_norm_g, w_uq, w_ukv, w_c_out, w_o, ln_g, ln_b):
    wi = w_in[l]
    o_a = 0
    o_b = A_COLS
    o_q = o_b + B_COLS
    o_kv = o_q + Q_LORA
    o_kr = o_kv + KV_LORA
    o_cg = o_kr + QK_ROPE
    zeros = lambda n: jnp.zeros((D_MODEL, n), w_in.dtype)
    wc = jnp.concatenate(
        [wi[:, o_q:o_kr], zeros(QK_NOPE), wi[:, o_kr:o_cg], zeros(HEAD_PAD - QK_NOPE - QK_ROPE)], axis=1)
    wuq = w_uq[l].reshape(Q_LORA, MLA_HEADS, QK_NOPE + QK_ROPE)
    wuq = jnp.pad(wuq, ((0, 0), (0, 0), (0, HEAD_PAD - QK_NOPE - QK_ROPE))).reshape(Q_LORA, QK_PAD)
    wukv = w_ukv[l].reshape(KV_LORA, MLA_HEADS, QK_NOPE + V_HEAD)
    wk = jnp.pad(wukv[:, :, :QK_NOPE], ((0, 0), (0, 0), (0, HEAD_PAD - QK_NOPE))).reshape(KV_LORA, QK_PAD)
    wv = wukv[:, :, QK_NOPE:].reshape(KV_LORA, MLA_HEADS // 2, 2, V_HEAD)
    zv = jnp.zeros_like(wv[:, :, 0])
    wv = jnp.stack([wv[:, :, 0], zv, zv, wv[:, :, 1]], axis=2).reshape(KV_LORA, QK_PAD)
    row = lambda a: a[l].reshape(1, -1)
    return {
        "wa": wi[:, o_a:o_b].astype(_BF),
        "wb": wi[:, o_b:o_q].astype(_BF),
        "wc": wc.astype(_BF),
        "w3": wi[:, o_cg:].astype(_BF),
        "conv_a_w": conv_a_w[l], "conv_a_b": row(conv_a_b), "ln_a_g": row(ln_a_g), "ln_a_b": row(ln_a_b),
        "conv_b_w": conv_b_w[l], "q_norm_g": row(q_norm_g), "kv_norm_g": row(kv_norm_g),
        "wuq": wuq.astype(_BF),
        "wukv": jnp.concatenate([wk, wv], axis=1).astype(_BF),
        "w_a_out": w_a_out[l].astype(_BF), "w_b_out": w_b_out[l].astype(_BF),
        "w_c_out": w_c_out[l].astype(_BF), "w_o": w_o[l].astype(_BF),
        "ln_g": row(ln_g), "ln_b": row(ln_b),
    }


def kernel(x, c, positions, w_ada, b_ada, w_in, conv_a_w, conv_a_b, ln_a_g, ln_a_b, w_a_out, conv_b_w, w_b_out, q_norm_g, kv_norm_g, w_uq, w_ukv, w_c_out, w_o, ln_g, ln_b):
    assert x.shape == (BATCH, SEQ, D_MODEL) and x.dtype == _F32
    ada = _ada_all_layers(c, w_ada, b_ada)
    c_tab, s_tab = _rope_tables(positions)
    h = x.reshape(BATCH * SEQ, D_MODEL)
    for l in range(DEPTH):
        lw = _layer_weights(l, w_in, conv_a_w, conv_a_b, ln_a_g, ln_a_b, w_a_out, conv_b_w, w_b_out,
                            q_norm_g, kv_norm_g, w_uq, w_ukv, w_c_out, w_o, ln_g, ln_b)
        shift = ada[l, 0].reshape(BATCH, 1, D_MODEL)
        scale = ada[l, 1].reshape(BATCH, 1, D_MODEL)
        gate = ada[l, 2].reshape(BATCH, 1, D_MODEL)
        za, zb, q, k, v = _front(h, shift, scale, lw, c_tab, s_tab)
        o = _attention(q, k, v)
        h = _back(h, shift, scale, gate, za, zb, o, lw)
    return h.reshape(BATCH, SEQ, D_MODEL)
```

```python
import functools

import jax
import jax.numpy as jnp
import numpy as np
from jax import lax
from jax.experimental import pallas as pl
from jax.experimental.pallas import tpu as pltpu

D_MODEL = 1024
BATCH = 8
SEQ = 2048
DEPTH = 4
CONF_WIDTH = 512
CONF_KERNEL = 31
SC_WIDTH = 512
SC_KERNEL = 3
MLA_HEADS = 8
QK_NOPE = 64
QK_ROPE = 32
V_HEAD = 64
Q_LORA = 384
KV_LORA = 256
MLA_WIDTH = MLA_HEADS * V_HEAD
ROPE_THETA = 10000.0
N_BRANCH = 3
LN_EPS = 1e-5
RMS_EPS = 1e-6
DEEPNORM_ALPHA = (2 * DEPTH) ** 0.25

LANES = 128
HEAD_PAD = LANES
HALF_ROPE = QK_ROPE // 2
QK_PAD = MLA_HEADS * HEAD_PAD
A_COLS = 3 * CONF_WIDTH
B_COLS = 4 * SC_WIDTH
C_COLS = Q_LORA + KV_LORA + HEAD_PAD
BACK_COLS = MLA_WIDTH + N_BRANCH * D_MODEL
KV_UP_COLS = 2 * QK_PAD

ROW_TILE = 256
CONV_CHUNK = 32
SUBLANES = 8
N_SEG = SUBLANES
SEG = ROW_TILE // N_SEG
PITCH = SEG + 4
N_SLAB = CONF_WIDTH // LANES
CONV_GROUP = 8
B_HALO = 8
ATT_TILE = 256
ATT_HEADS = 8
VMEM_LIMIT = 48 * 1024 * 1024

_BF = jnp.bfloat16
_F32 = jnp.float32


def _dot(a, b):
    return jnp.dot(a, b, preferred_element_type=_F32)


def _sigmoid(x):
    return jax.nn.sigmoid(x)


def _silu(x):
    return x * jax.nn.sigmoid(x)


def _const_spec(shape):
    n = len(shape)
    return pl.BlockSpec(shape, lambda b, s: (0,) * n)


def _ada_kernel(c_ref, w_ref, b_ref, o_ref):
    c = c_ref[...]
    c_act = c * jax.nn.sigmoid(c)
    o_ref[...] = _dot(c_act, w_ref[...]) + b_ref[...]


def _ada_all_layers(c, w_ada, b_ada):
    b_r = b_ada.reshape(DEPTH, 3, 1, D_MODEL)
    return pl.pallas_call(
        _ada_kernel,
        grid=(DEPTH, 3),
        in_specs=[
            pl.BlockSpec((BATCH, D_MODEL), lambda l, j: (0, 0)),
            pl.BlockSpec((None, D_MODEL, D_MODEL), lambda l, j: (l, 0, j)),
            pl.BlockSpec((None, None, 1, D_MODEL), lambda l, j: (l, j, 0, 0)),
        ],
        out_specs=pl.BlockSpec((None, None, BATCH, D_MODEL), lambda l, j: (l, j, 0, 0)),
        out_shape=jax.ShapeDtypeStruct((DEPTH, 3, BATCH, D_MODEL), _F32),
        name="ada",
    )(c, w_ada, b_r)


def _angle_kernel(pos_ref, invf_ref, cos_ref, sin_ref):
    ang = pos_ref[...].astype(_F32) * invf_ref[...]
    cos_ref[...] = jnp.cos(ang)
    sin_ref[...] = jnp.sin(ang)


def _expand_kernel(cos_ref, sin_ref, ec_ref, es_ref, base_ref, c_out, s_out):
    hi = lax.Precision.HIGHEST
    c_out[...] = jnp.dot(cos_ref[...], ec_ref[...], precision=hi, preferred_element_type=_F32) + base_ref[...]
    s_out[...] = jnp.dot(sin_ref[...], es_ref[...], precision=hi, preferred_element_type=_F32)


def _rope_tables(positions):
    n_tok = BATCH * SEQ
    rows = n_tok * HALF_ROPE // LANES
    inv_freq = ROPE_THETA ** (-jnp.arange(0, QK_ROPE, 2, dtype=_F32) / QK_ROPE)
    pos_rep = jnp.repeat(positions.reshape(-1), HALF_ROPE).reshape(rows, LANES)
    invf_rep = jnp.tile(inv_freq, LANES // HALF_ROPE).reshape(1, LANES)
    cos_d, sin_d = pl.pallas_call(
        _angle_kernel,
        out_shape=[jax.ShapeDtypeStruct((rows, LANES), _F32)] * 2,
        name="rope_angles",
    )(pos_rep, invf_rep)
    cos16 = cos_d.reshape(n_tok, HALF_ROPE)
    sin16 = sin_d.reshape(n_tok, HALF_ROPE)

    eye = np.eye(HALF_ROPE, dtype=np.float32)
    ec = np.zeros((HALF_ROPE, LANES), np.float32)
    es = np.zeros((HALF_ROPE, LANES), np.float32)
    ec[:, QK_NOPE:QK_NOPE + HALF_ROPE] = eye
    ec[:, QK_NOPE + HALF_ROPE:QK_NOPE + QK_ROPE] = eye
    es[:, QK_NOPE:QK_NOPE + HALF_ROPE] = -eye
    es[:, QK_NOPE + HALF_ROPE:QK_NOPE + QK_ROPE] = eye
    base = np.ones((1, LANES), np.float32)
    base[:, QK_NOPE:QK_NOPE + QK_ROPE] = 0.0

    blk = 2048
    return pl.pallas_call(
        _expand_kernel,
        grid=(n_tok // blk,),
        in_specs=[
            pl.BlockSpec((blk, HALF_ROPE), lambda i: (i, 0)),
            pl.BlockSpec((blk, HALF_ROPE), lambda i: (i, 0)),
            pl.BlockSpec((HALF_ROPE, LANES), lambda i: (0, 0)),
            pl.BlockSpec((HALF_ROPE, LANES), lambda i: (0, 0)),
            pl.BlockSpec((1, LANES), lambda i: (0, 0)),
        ],
        out_specs=[pl.BlockSpec((blk, LANES), lambda i: (i, 0))] * 2,
        out_shape=[jax.ShapeDtypeStruct((n_tok, LANES), _F32)] * 2,
        name="rope_expand",
    )(cos16, sin16, jnp.asarray(ec), jnp.asarray(es), jnp.asarray(base))


def _rope128(t, c_tab, s_tab, lo_half):
    swapped = jnp.where(lo_half, pltpu.roll(t, LANES - HALF_ROPE, 1), pltpu.roll(t, HALF_ROPE, 1))
    return t * c_tab + swapped * s_tab


def _front_kernel(x_ref, shift_ref, scale_ref, wa_ref, wb_ref, wc_ref,
                  cw_ref, cb_ref, lag_ref, lab_ref, cbw_ref, qg_ref, kvg_ref,
                  wuq_ref, wukv_ref, vones_ref, ctab_ref, stab_ref,
                  za_ref, zb_ref, q_ref, k_ref, v_ref,
                  pa_ref, pb_ref, abuf_ref, ybuf_ref, bbuf_ref):
    tm = ROW_TILE

    @pl.when(pl.program_id(1) == 0)
    def _():
        abuf_ref[:, 0:SEG, :] = jnp.zeros((N_SLAB, SEG, LANES), _F32)
        bbuf_ref[0:B_HALO, :] = jnp.zeros((B_HALO, SC_WIDTH), _F32)

    u = (x_ref[...] * (1.0 + scale_ref[...]) + shift_ref[...]).astype(_BF)

    pa_ref[...] = _dot(u, wa_ref[...])
    glu = pa_ref[:, 0:CONF_WIDTH] * _sigmoid(pa_ref[:, CONF_WIDTH:2 * CONF_WIDTH])
    for s in range(N_SEG):
        for c in range(N_SLAB):
            abuf_ref[c, PITCH * (s + 1):PITCH * (s + 1) + SEG, :] = (
                glu[SEG * s:SEG * (s + 1), c * LANES:(c + 1) * LANES])
    for c in range(N_SLAB):
        ls = slice(c * LANES, (c + 1) * LANES)
        bs = slice(c * B_COLS // N_SLAB, (c + 1) * B_COLS // N_SLAB)
        pb_ref[:, bs] = _dot(u, wb_ref[:, bs])
        for j0 in range(0, SEG, CONV_GROUP):
            accs = [cb_ref[:, ls]] * CONV_GROUP
            for k in range(CONF_KERNEL):
                w = cw_ref[k, :, ls]
                for jj in range(CONV_GROUP):
                    i = j0 + jj - (CONF_KERNEL - 1) + k
                    start = PITCH + i if i >= 0 else PITCH - (PITCH - SEG) + i
                    accs[jj] = accs[jj] + w * abuf_ref[c, pl.ds(start, N_SEG, stride=PITCH), :]
            for jj in range(CONV_GROUP):
                ybuf_ref[c, pl.ds(j0 + jj, N_SEG, stride=PITCH), :] = accs[jj]
        abuf_ref[c, 0:SEG, :] = abuf_ref[c, PITCH * N_SEG:PITCH * N_SEG + SEG, :]
    for s in range(N_SEG):
        rows = slice(SEG * s, SEG * (s + 1))
        conv = jnp.concatenate(
            [ybuf_ref[c, PITCH * s:PITCH * s + SEG, :] for c in range(N_SLAB)], axis=1)
        mu = jnp.mean(conv, axis=-1, keepdims=True)
        cen = conv - mu
        var = jnp.mean(cen * cen, axis=-1, keepdims=True)
        a_n = cen * lax.rsqrt(var + LN_EPS) * lag_ref[...] + lab_ref[...]
        gate = pa_ref[rows, 2 * CONF_WIDTH:3 * CONF_WIDTH]
        za_ref[rows, :] = (_silu(a_n) * _silu(gate)).astype(_BF)

    bbuf_ref[B_HALO:B_HALO + tm, :] = pb_ref[:, 0:SC_WIDTH] * pb_ref[:, 2 * SC_WIDTH:3 * SC_WIDTH]
    firstb = B_HALO - (SC_KERNEL - 1)
    for r in range(0, tm, CONV_CHUNK):
        acc = cbw_ref[0:1, :] * bbuf_ref[firstb + r:firstb + r + CONV_CHUNK, :]
        for k in range(1, SC_KERNEL):
            acc = acc + cbw_ref[k:k + 1, :] * bbuf_ref[firstb + k + r:firstb + k + r + CONV_CHUNK, :]
        yb = pb_ref[r:r + CONV_CHUNK, SC_WIDTH:2 * SC_WIDTH] * acc
        zb_ref[r:r + CONV_CHUNK, :] = (yb * _silu(pb_ref[r:r + CONV_CHUNK, 3 * SC_WIDTH:4 * SC_WIDTH])).astype(_BF)
    bbuf_ref[0:B_HALO, :] = bbuf_ref[tm:tm + B_HALO, :]

    pc = _dot(u, wc_ref[...])
    q_lat = pc[:, 0:Q_LORA]
    kv_lat = pc[:, Q_LORA:Q_LORA + KV_LORA]
    k_rope = pc[:, Q_LORA + KV_LORA:C_COLS]
    qn = q_lat * lax.rsqrt(jnp.mean(q_lat * q_lat, axis=-1, keepdims=True) + RMS_EPS) * qg_ref[...]
    kvn = kv_lat * lax.rsqrt(jnp.mean(kv_lat * kv_lat, axis=-1, keepdims=True) + RMS_EPS) * kvg_ref[...]
    q_up = _dot(qn.astype(_BF), wuq_ref[...])
    kv_up = _dot(kvn.astype(_BF), wukv_ref[...])

    c_tab = ctab_ref[...]
    s_tab = stab_ref[...]
    lane = lax.broadcasted_iota(jnp.int32, (tm, LANES), 1)
    lo_half = lane < QK_NOPE + HALF_ROPE
    sm_scale = (QK_NOPE + QK_ROPE) ** -0.5 * np.log2(np.e)
    k_pe = _rope128(k_rope, c_tab, s_tab, lo_half)
    for h in range(MLA_HEADS):
        sl = slice(h * HEAD_PAD, (h + 1) * HEAD_PAD)
        q_ref[:, sl] = (_rope128(q_up[:, sl], c_tab, s_tab, lo_half) * sm_scale).astype(_BF)
        k_ref[:, sl] = (kv_up[:, sl] + k_pe).astype(_BF)
    v_ref[...] = (kv_up[:, QK_PAD:KV_UP_COLS] + vones_ref[...]).astype(_BF)


def _v_ones_row():
    pair = np.concatenate([np.zeros(V_HEAD), np.ones(2 * V_HEAD), np.zeros(V_HEAD)]).astype(np.float32)
    return jnp.asarray(np.tile(pair, MLA_HEADS // 2).reshape(1, QK_PAD))


def _front(x, shift, scale, lw, c_tab, s_tab):
    tm = ROW_TILE
    n_s = SEQ // tm
    row = lambda b, s: (b * n_s + s, 0)
    per_b = lambda b, s: (b, 0, 0)
    in_specs = [
        pl.BlockSpec((tm, D_MODEL), row),
        pl.BlockSpec((None, 1, D_MODEL), per_b),
        pl.BlockSpec((None, 1, D_MODEL), per_b),
        _const_spec((D_MODEL, A_COLS)),
        _const_spec((D_MODEL, B_COLS)),
        _const_spec((D_MODEL, C_COLS)),
        _const_spec((CONF_KERNEL, SUBLANES, CONF_WIDTH)),
        _const_spec((SUBLANES, CONF_WIDTH)),
        _const_spec((1, CONF_WIDTH)),
        _const_spec((1, CONF_WIDTH)),
        _const_spec((SC_KERNEL, SC_WIDTH)),
        _const_spec((1, Q_LORA)),
        _const_spec((1, KV_LORA)),
        _const_spec((Q_LORA, QK_PAD)),
        _const_spec((KV_LORA, KV_UP_COLS)),
        _const_spec((1, QK_PAD)),
        pl.BlockSpec((tm, LANES), row),
        pl.BlockSpec((tm, LANES), row),
    ]
    n_tok = BATCH * SEQ
    out_shape = [
        jax.ShapeDtypeStruct((n_tok, CONF_WIDTH), _BF),
        jax.ShapeDtypeStruct((n_tok, SC_WIDTH), _BF),
        jax.ShapeDtypeStruct((n_tok, QK_PAD), _BF),
        jax.ShapeDtypeStruct((n_tok, QK_PAD), _BF),
        jax.ShapeDtypeStruct((n_tok, QK_PAD), _BF),
    ]
    out_specs = [
        pl.BlockSpec((tm, CONF_WIDTH), row),
        pl.BlockSpec((tm, SC_WIDTH), row),
        pl.BlockSpec((tm, QK_PAD), row),
        pl.BlockSpec((tm, QK_PAD), row),
        pl.BlockSpec((tm, QK_PAD), row),
    ]
    return pl.pallas_call(
        _front_kernel,
        grid=(BATCH, n_s),
        in_specs=in_specs,
        out_specs=out_specs,
        out_shape=out_shape,
        scratch_shapes=[
            pltpu.VMEM((tm, A_COLS), _F32),
            pltpu.VMEM((tm, B_COLS), _F32),
            pltpu.VMEM((N_SLAB, -(-(N_SEG + 1) * PITCH // SUBLANES) * SUBLANES, LANES), _F32),
            pltpu.VMEM((N_SLAB, N_SEG * PITCH, LANES), _F32),
            pltpu.VMEM((tm + B_HALO, SC_WIDTH), _F32),
        ],
        compiler_params=pltpu.CompilerParams(
            dimension_semantics=("arbitrary", "arbitrary"), vmem_limit_bytes=VMEM_LIMIT),
        name="front",
    )(x, shift, scale, lw["wa"], lw["wb"], lw["wc"], lw["conv_a_w"], lw["conv_a_b"],
      lw["ln_a_g"], lw["ln_a_b"], lw["conv_b_w"], lw["q_norm_g"], lw["kv_norm_g"],
      lw["wuq"], lw["wukv"], _v_ones_row(), c_tab, s_tab)


def _attn_kernel(q_ref, k_ref, v_ref, o_ref, s_buf, mx_ref, acc_ref):
    t = ATT_TILE
    qi = pl.program_id(2)
    nt = (((1,), (1,)), ((), ()))
    heads = range(ATT_HEADS)
    hs = [slice(h * HEAD_PAD, (h + 1) * HEAD_PAD) for h in heads]

    def scores(j, h):
        kb = k_ref[pl.ds(pl.multiple_of(j * t, t), t), hs[h]]
        return lax.dot_general(q_ref[:, hs[h]], kb, nt, preferred_element_type=_F32)

    def lane_max(s):
        out = s[:, 0:LANES]
        for c in range(1, t // LANES):
            out = jnp.maximum(out, s[:, c * LANES:(c + 1) * LANES])
        return out

    for h in heads:
        mx_ref[h] = jnp.full((t, LANES), -jnp.inf, _F32)
        acc_ref[h] = jnp.zeros((t, LANES), _F32)

    def pass1(j0, n_blocks):
        for h in heads:
            mx = mx_ref[h]
            for b in range(n_blocks):
                s = scores(j0 + b, h)
                s_buf[j0 + b, h] = s
                mx = jnp.maximum(mx, lane_max(s))
            mx_ref[h] = mx

    def pass1_pair(jj, carry):
        pass1(2 * jj, 2)
        return carry

    lax.fori_loop(0, qi // 2, pass1_pair, 0)

    @pl.when(qi % 2 == 1)
    def _():
        pass1(qi - 1, 1)

    row = lax.broadcasted_iota(jnp.int32, (t, t), 0)
    col = lax.broadcasted_iota(jnp.int32, (t, t), 1)
    causal = col <= row
    for h in heads:
        s = jnp.where(causal, scores(qi, h), -jnp.inf)
        s_buf[qi, h] = s
        mx_ref[h] = jnp.maximum(mx_ref[h], lane_max(s))

    m = [jnp.max(mx_ref[h], axis=-1, keepdims=True) for h in heads]

    def pass2(j0, n_blocks):
        vbs = [v_ref[pl.ds(pl.multiple_of((j0 + b) * t, t), t), :] for b in range(n_blocks)]
        for h in heads:
            acc = acc_ref[h]
            for b in range(n_blocks):
                p = jnp.exp2(s_buf[j0 + b, h] - m[h]).astype(_BF)
                acc = acc + _dot(p, vbs[b][:, hs[h]])
            acc_ref[h] = acc

    def pass2_pair(jj, carry):
        pass2(2 * jj, 2)
        return carry

    n_keys = qi + 1
    lax.fori_loop(0, n_keys // 2, pass2_pair, 0)

    @pl.when(n_keys % 2 == 1)
    def _():
        pass2(qi, 1)

    lane = lax.broadcasted_iota(jnp.int32, (t, LANES), 1)
    for g in range(ATT_HEADS // 2):
        even = acc_ref[2 * g]
        odd = acc_ref[2 * g + 1]
        den = jnp.where(lane < V_HEAD, pltpu.roll(even, V_HEAD, 1), pltpu.roll(odd, V_HEAD, 1))
        num = jnp.where(lane < V_HEAD, even, odd)
        o_ref[:, g * LANES:(g + 1) * LANES] = (num / den).astype(o_ref.dtype)


def _attention(q, k, v):
    t = ATT_TILE
    n_q = SEQ // t
    n_grp = MLA_HEADS // ATT_HEADS
    q3 = q.reshape(BATCH, SEQ, QK_PAD)
    k3 = k.reshape(BATCH, SEQ, QK_PAD)
    v3 = v.reshape(BATCH, SEQ, QK_PAD)
    o = pl.pallas_call(
        _attn_kernel,
        grid=(BATCH, n_grp, n_q),
        in_specs=[
            pl.BlockSpec((None, t, ATT_HEADS * HEAD_PAD), lambda b, g, i: (b, i, g)),
            pl.BlockSpec((None, SEQ, ATT_HEADS * HEAD_PAD), lambda b, g, i: (b, 0, g)),
            pl.BlockSpec((None, SEQ, ATT_HEADS * HEAD_PAD), lambda b, g, i: (b, 0, g)),
        ],
        out_specs=pl.BlockSpec((None, t, ATT_HEADS * V_HEAD), lambda b, g, i: (b, i, g)),
        out_shape=jax.ShapeDtypeStruct((BATCH, SEQ, MLA_WIDTH), _BF),
        scratch_shapes=[
            pltpu.VMEM((n_q, ATT_HEADS, t, t), _F32),
            pltpu.VMEM((ATT_HEADS, t, LANES), _F32),
            pltpu.VMEM((ATT_HEADS, t, LANES), _F32),
        ],
        compiler_params=pltpu.CompilerParams(
            dimension_semantics=("arbitrary", "arbitrary", "arbitrary"), vmem_limit_bytes=VMEM_LIMIT),
        name="attn",
    )(q3, k3, v3)
    return o.reshape(BATCH * SEQ, MLA_WIDTH)


def _back_kernel(x_ref, shift_ref, scale_ref, gate_ref, za_ref, zb_ref, o_ref,
                 w3_ref, wao_ref, wbo_ref, wco_ref, wo_ref, lg_ref, lb_ref, out_ref):
    x = x_ref[...]
    u = (x * (1.0 + scale_ref[...]) + shift_ref[...]).astype(_BF)
    p3 = _dot(u, w3_ref[...])
    c_gate = p3[:, 0:MLA_WIDTH]
    oc = (o_ref[...].astype(_F32) * _silu(c_gate)).astype(_BF)
    m = _sigmoid(p3[:, MLA_WIDTH:MLA_WIDTH + D_MODEL]) * _dot(za_ref[...], wao_ref[...])
    m = m + _sigmoid(p3[:, MLA_WIDTH + D_MODEL:MLA_WIDTH + 2 * D_MODEL]) * _dot(zb_ref[...], wbo_ref[...])
    m = m + _sigmoid(p3[:, MLA_WIDTH + 2 * D_MODEL:BACK_COLS]) * _dot(oc, wco_ref[...])
    out = _dot(m.astype(_BF), wo_ref[...])
    r = DEEPNORM_ALPHA * x + gate_ref[...] * out
    mu = jnp.mean(r, axis=-1, keepdims=True)
    cen = r - mu
    var = jnp.mean(cen * cen, axis=-1, keepdims=True)
    out_ref[...] = cen * lax.rsqrt(var + LN_EPS) * lg_ref[...] + lb_ref[...]


def _back(x, shift, scale, gate, za, zb, o, lw):
    tm = ROW_TILE
    n_s = SEQ // tm
    row = lambda b, s: (b * n_s + s, 0)
    per_b = lambda b, s: (b, 0, 0)
    in_specs = [
        pl.BlockSpec((tm, D_MODEL), row),
        pl.BlockSpec((None, 1, D_MODEL), per_b),
        pl.BlockSpec((None, 1, D_MODEL), per_b),
        pl.BlockSpec((None, 1, D_MODEL), per_b),
        pl.BlockSpec((tm, CONF_WIDTH), row),
        pl.BlockSpec((tm, SC_WIDTH), row),
        pl.BlockSpec((tm, MLA_WIDTH), row),
        _const_spec((D_MODEL, BACK_COLS)),
        _const_spec((CONF_WIDTH, D_MODEL)),
        _const_spec((SC_WIDTH, D_MODEL)),
        _const_spec((MLA_WIDTH, D_MODEL)),
        _const_spec((D_MODEL, D_MODEL)),
        _const_spec((1, D_MODEL)),
        _const_spec((1, D_MODEL)),
    ]
    return pl.pallas_call(
        _back_kernel,
        grid=(BATCH, n_s),
        in_specs=in_specs,
        out_specs=pl.BlockSpec((tm, D_MODEL), row),
        out_shape=jax.ShapeDtypeStruct((BATCH * SEQ, D_MODEL), _F32),
        compiler_params=pltpu.CompilerParams(
            dimension_semantics=("arbitrary", "arbitrary"), vmem_limit_bytes=VMEM_LIMIT),
        name="back",
    )(x, shift, scale, gate, za, zb, o, lw["w3"], lw["w_a_out"], lw["w_b_out"], lw["w_c_out"],
      lw["w_o"], lw["ln_g"], lw["ln_b"])


def _layer_weights(l, w_in, conv_a_w, conv_a_b, ln_a_g, ln_a_b, w_a_out, conv_b_w, w_b_out,
                   q_norm_g, kv_norm_g, w_uq, w_ukv, w_c_out, w_o, ln_g, ln_b):
    wi = w_in[l]
    o_a = 0
    o_b = A_COLS
    o_q = o_b + B_COLS
    o_kv = o_q + Q_LORA
    o_kr = o_kv + KV_LORA
    o_cg = o_kr + QK_ROPE
    zeros = lambda n: jnp.zeros((D_MODEL, n), w_in.dtype)
    wc = jnp.concatenate(
        [wi[:, o_q:o_kr], zeros(QK_NOPE), wi[:, o_kr:o_cg], zeros(HEAD_PAD - QK_NOPE - QK_ROPE)], axis=1)
    wuq = w_uq[l].reshape(Q_LORA, MLA_HEADS, QK_NOPE + QK_ROPE)
    wuq = jnp.pad(wuq, ((0, 0), (0, 0), (0, HEAD_PAD - QK_NOPE - QK_ROPE))).reshape(Q_LORA, QK_PAD)
    wukv = w_ukv[l].reshape(KV_LORA, MLA_HEADS, QK_NOPE + V_HEAD)
    wk = jnp.pad(wukv[:, :, :QK_NOPE], ((0, 0), (0, 0), (0, HEAD_PAD - QK_NOPE))).reshape(KV_LORA, QK_PAD)
    wv = wukv[:, :, QK_NOPE:].reshape(KV_LORA, MLA_HEADS // 2, 2, V_HEAD)
    zv = jnp.zeros_like(wv[:, :, 0])
    wv = jnp.stack([wv[:, :, 0], zv, zv, wv[:, :, 1]], axis=2).reshape(KV_LORA, QK_PAD)
    row = lambda a: a[l].reshape(1, -1)
    return {
        "wa": wi[:, o_a:o_b].astype(_BF),
        "wb": wi[:, o_b:o_q].astype(_BF),
        "wc": wc.astype(_BF),
        "w3": wi[:, o_cg:].astype(_BF),
        "conv_a_w": jnp.broadcast_to(conv_a_w[l][:, None, :], (CONF_KERNEL, SUBLANES, CONF_WIDTH)),
        "conv_a_b": jnp.broadcast_to(row(conv_a_b), (SUBLANES, CONF_WIDTH)), "ln_a_g": row(ln_a_g), "ln_a_b": row(ln_a_b),
        "conv_b_w": conv_b_w[l], "q_norm_g": row(q_norm_g), "kv_norm_g": row(kv_norm_g),
        "wuq": wuq.astype(_BF),
        "wukv": jnp.concatenate([wk, wv], axis=1).astype(_BF),
        "w_a_out": w_a_out[l].astype(_BF), "w_b_out": w_b_out[l].astype(_BF),
        "w_c_out": w_c_out[l].astype(_BF), "w_o": w_o[l].astype(_BF),
        "ln_g": row(ln_g), "ln_b": row(ln_b),
    }


def kernel(x, c, positions, w_ada, b_ada, w_in, conv_a_w, conv_a_b, ln_a_g, ln_a_b, w_a_out, conv_b_w, w_b_out, q_norm_g, kv_norm_g, w_uq, w_ukv, w_c_out, w_o, ln_g, ln_b):
    assert x.shape == (BATCH, SEQ, D_MODEL) and x.dtype == _F32
    ada = _ada_all_layers(c, w_ada, b_ada)
    c_tab, s_tab = _rope_tables(positions)
    h = x.reshape(BATCH * SEQ, D_MODEL)
    for l in range(DEPTH):
        lw = _layer_weights(l, w_in, conv_a_w, conv_a_b, ln_a_g, ln_a_b, w_a_out, conv_b_w, w_b_out,
                            q_norm_g, kv_norm_g, w_uq, w_ukv, w_c_out, w_o, ln_g, ln_b)
        shift = ada[l, 0].reshape(BATCH, 1, D_MODEL)
        scale = ada[l, 1].reshape(BATCH, 1, D_MODEL)
        gate = ada[l, 2].reshape(BATCH, 1, D_MODEL)
        za, zb, q, k, v = _front(h, shift, scale, lw, c_tab, s_tab)
        o = _attention(q, k, v)
        h = _back(h, shift, scale, gate, za, zb, o, lw)
    return h.reshape(BATCH, SEQ, D_MODEL)
```

```python
import jax
import jax.numpy as jnp
import numpy as np
from jax import lax
from jax.experimental import pallas as pl
from jax.experimental.pallas import tpu as pltpu

D_MODEL = 1024
BATCH = 8
SEQ = 2048
DEPTH = 4
CONF_WIDTH = 512
CONF_KERNEL = 31
SC_WIDTH = 512
SC_KERNEL = 3
MLA_HEADS = 8
QK_NOPE = 64
QK_ROPE = 32
V_HEAD = 64
Q_LORA = 384
KV_LORA = 256
MLA_WIDTH = MLA_HEADS * V_HEAD
ROPE_THETA = 10000.0
N_BRANCH = 3
LN_EPS = 1e-5
RMS_EPS = 1e-6
DEEPNORM_ALPHA = (2 * DEPTH) ** 0.25

LANES = 128
SUBLANES = 8
HEAD_PAD = LANES
HALF_ROPE = QK_ROPE // 2
QK_PAD = MLA_HEADS * HEAD_PAD
A_COLS = 3 * CONF_WIDTH
B_COLS = 4 * SC_WIDTH
C_COLS = Q_LORA + KV_LORA + HEAD_PAD
BACK_COLS = MLA_WIDTH + N_BRANCH * D_MODEL
KV_UP_COLS = 2 * QK_PAD
IN_Q = A_COLS + B_COLS
IN_KR = IN_Q + Q_LORA + KV_LORA
IN_BACK = IN_KR + QK_ROPE
D_IN = IN_BACK + BACK_COLS

ROW_TILE = 256
CONV_CHUNK = 32
N_SEG = SUBLANES
SEG = ROW_TILE // N_SEG
PITCH = SEG + 4
N_SLAB = CONF_WIDTH // LANES
N_BQ = B_COLS // SC_WIDTH
CONV_GROUP = 8
B_HALO = 8
ATT_TILE = 256
ATT_HEADS = 8
VMEM_LIMIT = 48 * 1024 * 1024

_BF = jnp.bfloat16
_F32 = jnp.float32


def _dot(a, b):
    return jnp.dot(a, b, preferred_element_type=_F32)


def _sigmoid(x):
    return jax.nn.sigmoid(x)


def _silu(x):
    return x * jax.nn.sigmoid(x)


def _layer_spec(shape, l, col_block=0):
    zeros = (0,) * (len(shape) - 1)
    return pl.BlockSpec((None,) + tuple(shape), lambda b, s: (l,) + zeros + (col_block,))


def _ada_kernel(c_ref, w_ref, b_ref, o_ref):
    c = c_ref[...]
    c_act = c * jax.nn.sigmoid(c)
    o_ref[...] = _dot(c_act, w_ref[...]) + b_ref[...]


def _ada_all_layers(c, w_ada, b_ada):
    b_r = b_ada.reshape(DEPTH, 3, 1, D_MODEL)
    ada = pl.pallas_call(
        _ada_kernel,
        grid=(DEPTH, 3),
        in_specs=[
            pl.BlockSpec((BATCH, D_MODEL), lambda l, j: (0, 0)),
            pl.BlockSpec((None, D_MODEL, D_MODEL), lambda l, j: (l, 0, j)),
            pl.BlockSpec((None, None, 1, D_MODEL), lambda l, j: (l, j, 0, 0)),
        ],
        out_specs=pl.BlockSpec((None, None, BATCH, D_MODEL), lambda l, j: (l, j, 0, 0)),
        out_shape=jax.ShapeDtypeStruct((DEPTH, 3, BATCH, D_MODEL), _F32),
        name="ada",
    )(c, w_ada, b_r)
    return ada.reshape(DEPTH, 3, BATCH, 1, D_MODEL)


def _ada_spec(l, which):
    return pl.BlockSpec((None, None, None, 1, D_MODEL), lambda b, s: (l, which, b, 0, 0))


def _angle_kernel(pos_ref, invf_ref, cos_ref, sin_ref):
    ang = pos_ref[...].astype(_F32) * invf_ref[...]
    cos_ref[...] = jnp.cos(ang)
    sin_ref[...] = jnp.sin(ang)


def _expand_kernel(cos_ref, sin_ref, ec_ref, es_ref, base_ref, c_out, s_out):
    hi = lax.Precision.HIGHEST
    c_out[...] = jnp.dot(cos_ref[...], ec_ref[...], precision=hi, preferred_element_type=_F32) + base_ref[...]
    s_out[...] = jnp.dot(sin_ref[...], es_ref[...], precision=hi, preferred_element_type=_F32)


def _rope_tables(positions):
    n_tok = BATCH * SEQ
    rows = n_tok * HALF_ROPE // LANES
    inv_freq = ROPE_THETA ** (-jnp.arange(0, QK_ROPE, 2, dtype=_F32) / QK_ROPE)
    pos_rep = jnp.repeat(positions.reshape(-1), HALF_ROPE).reshape(rows, LANES)
    invf_rep = jnp.tile(inv_freq, LANES // HALF_ROPE).reshape(1, LANES)
    cos_d, sin_d = pl.pallas_call(
        _angle_kernel,
        out_shape=[jax.ShapeDtypeStruct((rows, LANES), _F32)] * 2,
        name="rope_angles",
    )(pos_rep, invf_rep)
    cos16 = cos_d.reshape(n_tok, HALF_ROPE)
    sin16 = sin_d.reshape(n_tok, HALF_ROPE)

    eye = np.eye(HALF_ROPE, dtype=np.float32)
    ec = np.zeros((HALF_ROPE, LANES), np.float32)
    es = np.zeros((HALF_ROPE, LANES), np.float32)
    ec[:, QK_NOPE:QK_NOPE + HALF_ROPE] = eye
    ec[:, QK_NOPE + HALF_ROPE:QK_NOPE + QK_ROPE] = eye
    es[:, QK_NOPE:QK_NOPE + HALF_ROPE] = -eye
    es[:, QK_NOPE + HALF_ROPE:QK_NOPE + QK_ROPE] = eye
    base = np.ones((1, LANES), np.float32)
    base[:, QK_NOPE:QK_NOPE + QK_ROPE] = 0.0

    blk = 2048
    return pl.pallas_call(
        _expand_kernel,
        grid=(n_tok // blk,),
        in_specs=[
            pl.BlockSpec((blk, HALF_ROPE), lambda i: (i, 0)),
            pl.BlockSpec((blk, HALF_ROPE), lambda i: (i, 0)),
            pl.BlockSpec((HALF_ROPE, LANES), lambda i: (0, 0)),
            pl.BlockSpec((HALF_ROPE, LANES), lambda i: (0, 0)),
            pl.BlockSpec((1, LANES), lambda i: (0, 0)),
        ],
        out_specs=[pl.BlockSpec((blk, LANES), lambda i: (i, 0))] * 2,
        out_shape=[jax.ShapeDtypeStruct((n_tok, LANES), _F32)] * 2,
        name="rope_expand",
    )(cos16, sin16, jnp.asarray(ec), jnp.asarray(es), jnp.asarray(base))


def _rope128(t, c_tab, s_tab, lo_half):
    swapped = jnp.where(lo_half, pltpu.roll(t, LANES - HALF_ROPE, 1), pltpu.roll(t, HALF_ROPE, 1))
    return t * c_tab + swapped * s_tab


def _front_kernel(x_ref, shift_ref, scale_ref, wa_ref, wb0_ref, wb1_ref, wb2_ref, wb3_ref, wc_ref,
                  cw_ref, cb_ref, lag_ref, lab_ref, cbw_ref, qg_ref, kvg_ref,
                  wuq_ref, wukv_ref, vones_ref, ctab_ref, stab_ref,
                  za_ref, zb_ref, q_ref, k_ref, v_ref,
                  u_ref, pa_ref, pb_ref, abuf_ref, ybuf_ref, bbuf_ref,
                  qn_ref, kvn_ref, kr_ref, qup_ref, kvup_ref):
    tm = ROW_TILE

    @pl.when(pl.program_id(1) == 0)
    def _():
        abuf_ref[:, 0:SEG, :] = jnp.zeros((N_SLAB, SEG, LANES), _F32)
        bbuf_ref[0:B_HALO, :] = jnp.zeros((B_HALO, SC_WIDTH), _F32)

    u_ref[...] = (x_ref[...] * (1.0 + scale_ref[...]) + shift_ref[...]).astype(_BF)
    pa_ref[...] = _dot(u_ref[...], wa_ref[...])

    glu = pa_ref[:, 0:CONF_WIDTH] * _sigmoid(pa_ref[:, CONF_WIDTH:2 * CONF_WIDTH])
    for s in range(N_SEG):
        for c in range(N_SLAB):
            abuf_ref[c, PITCH * (s + 1):PITCH * (s + 1) + SEG, :] = (
                glu[SEG * s:SEG * (s + 1), c * LANES:(c + 1) * LANES])
    pc = _dot(u_ref[...], wc_ref[...])
    q_lat = pc[:, 0:Q_LORA]
    kv_lat = pc[:, Q_LORA:Q_LORA + KV_LORA]
    kr_ref[...] = pc[:, Q_LORA + KV_LORA:C_COLS]
    qn = q_lat * lax.rsqrt(jnp.mean(q_lat * q_lat, axis=-1, keepdims=True) + RMS_EPS) * qg_ref[...]
    kvn = kv_lat * lax.rsqrt(jnp.mean(kv_lat * kv_lat, axis=-1, keepdims=True) + RMS_EPS) * kvg_ref[...]
    qn_ref[...] = qn.astype(_BF)
    kvn_ref[...] = kvn.astype(_BF)
    wb_refs = (wb0_ref, wb1_ref, wb2_ref, wb3_ref)
    for c in range(N_SLAB):
        pb_ref[c] = _dot(u_ref[...], wb_refs[c][...])
        for j0 in range(0, SEG, CONV_GROUP):
            accs = [cb_ref[c]] * CONV_GROUP
            for k in range(CONF_KERNEL):
                w = cw_ref[c, k]
                for jj in range(CONV_GROUP):
                    i = j0 + jj - (CONF_KERNEL - 1) + k
                    start = PITCH + i if i >= 0 else PITCH - (PITCH - SEG) + i
                    accs[jj] = accs[jj] + w * abuf_ref[c, pl.ds(start, N_SEG, stride=PITCH), :]
            for jj in range(CONV_GROUP):
                ybuf_ref[c, pl.ds(j0 + jj, N_SEG, stride=PITCH), :] = accs[jj]
        abuf_ref[c, 0:SEG, :] = abuf_ref[c, PITCH * N_SEG:PITCH * N_SEG + SEG, :]
    qup_ref[...] = _dot(qn_ref[...], wuq_ref[...])
    kvup_ref[...] = _dot(kvn_ref[...], wukv_ref[...])
    for s in range(N_SEG):
        rows = slice(SEG * s, SEG * (s + 1))
        conv = jnp.concatenate(
            [ybuf_ref[c, PITCH * s:PITCH * s + SEG, :] for c in range(N_SLAB)], axis=1)
        mu = jnp.mean(conv, axis=-1, keepdims=True)
        cen = conv - mu
        var = jnp.mean(cen * cen, axis=-1, keepdims=True)
        a_n = cen * lax.rsqrt(var + LN_EPS) * lag_ref[...] + lab_ref[...]
        gate = pa_ref[rows, 2 * CONF_WIDTH:3 * CONF_WIDTH]
        za_ref[rows, :] = (_silu(a_n) * _silu(gate)).astype(_BF)

    bbuf_ref[B_HALO:B_HALO + tm, :] = pb_ref[0] * pb_ref[2]
    firstb = B_HALO - (SC_KERNEL - 1)
    for r in range(0, tm, CONV_CHUNK):
        acc = cbw_ref[0:1, :] * bbuf_ref[firstb + r:firstb + r + CONV_CHUNK, :]
        for k in range(1, SC_KERNEL):
            acc = acc + cbw_ref[k:k + 1, :] * bbuf_ref[firstb + k + r:firstb + k + r + CONV_CHUNK, :]
        yb = pb_ref[1, r:r + CONV_CHUNK, :] * acc
        zb_ref[r:r + CONV_CHUNK, :] = (yb * _silu(pb_ref[3, r:r + CONV_CHUNK, :])).astype(_BF)
    bbuf_ref[0:B_HALO, :] = bbuf_ref[tm:tm + B_HALO, :]

    c_tab = ctab_ref[...]
    s_tab = stab_ref[...]
    lane = lax.broadcasted_iota(jnp.int32, (tm, LANES), 1)
    lo_half = lane < QK_NOPE + HALF_ROPE
    sm_scale = (QK_NOPE + QK_ROPE) ** -0.5 * np.log2(np.e)
    k_pe = _rope128(kr_ref[...], c_tab, s_tab, lo_half)
    for h in range(MLA_HEADS):
        sl = slice(h * HEAD_PAD, (h + 1) * HEAD_PAD)
        q_ref[:, sl] = (_rope128(qup_ref[:, sl], c_tab, s_tab, lo_half) * sm_scale).astype(_BF)
        k_ref[:, sl] = (kvup_ref[:, sl] + k_pe).astype(_BF)
    v_ref[...] = (kvup_ref[:, QK_PAD:KV_UP_COLS] + vones_ref[...]).astype(_BF)


def _v_ones_row():
    pair = np.concatenate([np.zeros(V_HEAD), np.ones(2 * V_HEAD), np.zeros(V_HEAD)]).astype(np.float32)
    return jnp.asarray(np.tile(pair, MLA_HEADS // 2).reshape(1, QK_PAD))


def _front(x, ada, pw, l, c_tab, s_tab):
    tm = ROW_TILE
    n_s = SEQ // tm
    row = lambda b, s: (b * n_s + s, 0)
    in_specs = [
        pl.BlockSpec((tm, D_MODEL), row),
        _ada_spec(l, 0),
        _ada_spec(l, 1),
        _layer_spec((D_MODEL, A_COLS), l),
    ] + [
        _layer_spec((D_MODEL, SC_WIDTH), l, col_block=A_COLS // SC_WIDTH + part) for part in range(N_BQ)
    ] + [
        _layer_spec((D_MODEL, C_COLS), l),
        _layer_spec((N_SLAB, CONF_KERNEL, SUBLANES, LANES), l),
        _layer_spec((N_SLAB, SUBLANES, LANES), l),
        _layer_spec((1, CONF_WIDTH), l),
        _layer_spec((1, CONF_WIDTH), l),
        _layer_spec((SC_KERNEL, SC_WIDTH), l),
        _layer_spec((1, Q_LORA), l),
        _layer_spec((1, KV_LORA), l),
        _layer_spec((Q_LORA, QK_PAD), l),
        _layer_spec((KV_LORA, KV_UP_COLS), l),
        pl.BlockSpec((1, QK_PAD), lambda b, s: (0, 0)),
        pl.BlockSpec((tm, LANES), row),
        pl.BlockSpec((tm, LANES), row),
    ]
    n_tok = BATCH * SEQ
    out_widths = (CONF_WIDTH, SC_WIDTH, QK_PAD, QK_PAD, QK_PAD)
    return pl.pallas_call(
        _front_kernel,
        grid=(BATCH, n_s),
        in_specs=in_specs,
        out_specs=[pl.BlockSpec((tm, w), row) for w in out_widths],
        out_shape=[jax.ShapeDtypeStruct((n_tok, w), _BF) for w in out_widths],
        scratch_shapes=[
            pltpu.VMEM((tm, D_MODEL), _BF),
            pltpu.VMEM((tm, A_COLS), _F32),
            pltpu.VMEM((N_BQ, tm, SC_WIDTH), _F32),
            pltpu.VMEM((N_SLAB, -(-(N_SEG + 1) * PITCH // SUBLANES) * SUBLANES, LANES), _F32),
            pltpu.VMEM((N_SLAB, N_SEG * PITCH, LANES), _F32),
            pltpu.VMEM((tm + B_HALO, SC_WIDTH), _F32),
            pltpu.VMEM((tm, Q_LORA), _BF),
            pltpu.VMEM((tm, KV_LORA), _BF),
            pltpu.VMEM((tm, HEAD_PAD), _F32),
            pltpu.VMEM((tm, QK_PAD), _F32),
            pltpu.VMEM((tm, KV_UP_COLS), _F32),
        ],
        compiler_params=pltpu.CompilerParams(
            dimension_semantics=("arbitrary", "arbitrary"), vmem_limit_bytes=VMEM_LIMIT),
        name="front",
    )(x, ada, ada, pw["w_in"], pw["w_in"], pw["w_in"], pw["w_in"], pw["w_in"], pw["wc"],
      pw["conv_a_w"], pw["conv_a_b"], pw["ln_a_g"], pw["ln_a_b"], pw["conv_b_w"],
      pw["q_norm_g"], pw["kv_norm_g"], pw["wuq"], pw["wukv"], _v_ones_row(), c_tab, s_tab)


def _attn_kernel(q_ref, k_ref, v_ref, o_ref, s_buf, mx_ref, acc_ref):
    t = ATT_TILE
    qi = pl.program_id(2)
    nt = (((1,), (1,)), ((), ()))
    heads = range(ATT_HEADS)
    hs = [slice(h * HEAD_PAD, (h + 1) * HEAD_PAD) for h in heads]

    def scores(j, h):
        kb = k_ref[pl.ds(pl.multiple_of(j * t, t), t), hs[h]]
        return lax.dot_general(q_ref[:, hs[h]], kb, nt, preferred_element_type=_F32)

    def lane_max(s):
        out = s[:, 0:LANES]
        for c in range(1, t // LANES):
            out = jnp.maximum(out, s[:, c * LANES:(c + 1) * LANES])
        return out

    for h in heads:
        mx_ref[h] = jnp.full((t, LANES), -jnp.inf, _F32)
        acc_ref[h] = jnp.zeros((t, LANES), _F32)

    def pass1(j0, n_blocks):
        for h in heads:
            mx = mx_ref[h]
            for b in range(n_blocks):
                s = scores(j0 + b, h)
                s_buf[j0 + b, h] = s
                mx = jnp.maximum(mx, lane_max(s))
            mx_ref[h] = mx

    def pass1_pair(jj, carry):
        pass1(2 * jj, 2)
        return carry

    lax.fori_loop(0, qi // 2, pass1_pair, 0)

    @pl.when(qi % 2 == 1)
    def _():
        pass1(qi - 1, 1)

    row = lax.broadcasted_iota(jnp.int32, (t, t), 0)
    col = lax.broadcasted_iota(jnp.int32, (t, t), 1)
    causal = col <= row
    for h in heads:
        s = jnp.where(causal, scores(qi, h), -jnp.inf)
        s_buf[qi, h] = s
        mx_ref[h] = jnp.maximum(mx_ref[h], lane_max(s))

    m = [jnp.max(mx_ref[h], axis=-1, keepdims=True) for h in heads]

    def pass2(j0, n_blocks):
        vbs = [v_ref[pl.ds(pl.multiple_of((j0 + b) * t, t), t), :] for b in range(n_blocks)]
        for h in heads:
            acc = acc_ref[h]
            for b in range(n_blocks):
                p = jnp.exp2(s_buf[j0 + b, h] - m[h]).astype(_BF)
                acc = acc + _dot(p, vbs[b][:, hs[h]])
            acc_ref[h] = acc

    def pass2_pair(jj, carry):
        pass2(2 * jj, 2)
        return carry

    n_keys = qi + 1
    lax.fori_loop(0, n_keys // 2, pass2_pair, 0)

    @pl.when(n_keys % 2 == 1)
    def _():
        pass2(qi, 1)

    lane = lax.broadcasted_iota(jnp.int32, (t, LANES), 1)
    for g in range(ATT_HEADS // 2):
        even = acc_ref[2 * g]
        odd = acc_ref[2 * g + 1]
        den = jnp.where(lane < V_HEAD, pltpu.roll(even, V_HEAD, 1), pltpu.roll(odd, V_HEAD, 1))
        num = jnp.where(lane < V_HEAD, even, odd)
        o_ref[:, g * LANES:(g + 1) * LANES] = (num / den).astype(o_ref.dtype)


def _attention(q, k, v):
    t = ATT_TILE
    n_q = SEQ // t
    n_grp = MLA_HEADS // ATT_HEADS
    q3 = q.reshape(BATCH, SEQ, QK_PAD)
    k3 = k.reshape(BATCH, SEQ, QK_PAD)
    v3 = v.reshape(BATCH, SEQ, QK_PAD)
    o = pl.pallas_call(
        _attn_kernel,
        grid=(BATCH, n_grp, n_q),
        in_specs=[
            pl.BlockSpec((None, t, ATT_HEADS * HEAD_PAD), lambda b, g, i: (b, i, g)),
            pl.BlockSpec((None, SEQ, ATT_HEADS * HEAD_PAD), lambda b, g, i: (b, 0, g)),
            pl.BlockSpec((None, SEQ, ATT_HEADS * HEAD_PAD), lambda b, g, i: (b, 0, g)),
        ],
        out_specs=pl.BlockSpec((None, t, ATT_HEADS * V_HEAD), lambda b, g, i: (b, i, g)),
        out_shape=jax.ShapeDtypeStruct((BATCH, SEQ, MLA_WIDTH), _BF),
        scratch_shapes=[
            pltpu.VMEM((n_q, ATT_HEADS, t, t), _F32),
            pltpu.VMEM((ATT_HEADS, t, LANES), _F32),
            pltpu.VMEM((ATT_HEADS, t, LANES), _F32),
        ],
        compiler_params=pltpu.CompilerParams(
            dimension_semantics=("arbitrary", "arbitrary", "arbitrary"), vmem_limit_bytes=VMEM_LIMIT),
        name="attn",
    )(q3, k3, v3)
    return o.reshape(BATCH * SEQ, MLA_WIDTH)


def _back_kernel(x_ref, shift_ref, scale_ref, gate_ref, za_ref, zb_ref, o_ref,
                 w3_ref, wao_ref, wbo_ref, wco_ref, wo_ref, lg_ref, lb_ref, out_ref):
    x = x_ref[...]
    u = (x * (1.0 + scale_ref[...]) + shift_ref[...]).astype(_BF)
    p3 = _dot(u, w3_ref[...])
    c_gate = p3[:, 0:MLA_WIDTH]
    oc = (o_ref[...].astype(_F32) * _silu(c_gate)).astype(_BF)
    m = _sigmoid(p3[:, MLA_WIDTH:MLA_WIDTH + D_MODEL]) * _dot(za_ref[...], wao_ref[...])
    m = m + _sigmoid(p3[:, MLA_WIDTH + D_MODEL:MLA_WIDTH + 2 * D_MODEL]) * _dot(zb_ref[...], wbo_ref[...])
    m = m + _sigmoid(p3[:, MLA_WIDTH + 2 * D_MODEL:BACK_COLS]) * _dot(oc, wco_ref[...])
    out = _dot(m.astype(_BF), wo_ref[...])
    r = DEEPNORM_ALPHA * x + gate_ref[...] * out
    mu = jnp.mean(r, axis=-1, keepdims=True)
    cen = r - mu
    var = jnp.mean(cen * cen, axis=-1, keepdims=True)
    out_ref[...] = cen * lax.rsqrt(var + LN_EPS) * lg_ref[...] + lb_ref[...]


def _back(x, ada, za, zb, o, pw, l):
    tm = ROW_TILE
    n_s = SEQ // tm
    row = lambda b, s: (b * n_s + s, 0)
    in_specs = [
        pl.BlockSpec((tm, D_MODEL), row),
        _ada_spec(l, 0),
        _ada_spec(l, 1),
        _ada_spec(l, 2),
        pl.BlockSpec((tm, CONF_WIDTH), row),
        pl.BlockSpec((tm, SC_WIDTH), row),
        pl.BlockSpec((tm, MLA_WIDTH), row),
        _layer_spec((D_MODEL, BACK_COLS), l),
        _layer_spec((CONF_WIDTH, D_MODEL), l),
        _layer_spec((SC_WIDTH, D_MODEL), l),
        _layer_spec((MLA_WIDTH, D_MODEL), l),
        _layer_spec((D_MODEL, D_MODEL), l),
        _layer_spec((1, D_MODEL), l),
        _layer_spec((1, D_MODEL), l),
    ]
    return pl.pallas_call(
        _back_kernel,
        grid=(BATCH, n_s),
        in_specs=in_specs,
        out_specs=pl.BlockSpec((tm, D_MODEL), row),
        out_shape=jax.ShapeDtypeStruct((BATCH * SEQ, D_MODEL), _F32),
        compiler_params=pltpu.CompilerParams(
            dimension_semantics=("arbitrary", "arbitrary"), vmem_limit_bytes=VMEM_LIMIT),
        name="back",
    )(x, ada, ada, ada, za, zb, o, pw["w3"], pw["w_a_out"], pw["w_b_out"], pw["w_c_out"],
      pw["w_o"], pw["ln_g"], pw["ln_b"])


def _prepared_weights(w_in, conv_a_w, conv_a_b, ln_a_g, ln_a_b, w_a_out, conv_b_w, w_b_out,
                      q_norm_g, kv_norm_g, w_uq, w_ukv, w_c_out, w_o, ln_g, ln_b):
    w_in_bf = w_in.astype(_BF)
    zeros = lambda n: jnp.zeros((DEPTH, D_MODEL, n), _BF)
    wc = jnp.concatenate(
        [w_in_bf[:, :, IN_Q:IN_KR], zeros(QK_NOPE), w_in_bf[:, :, IN_KR:IN_BACK],
         zeros(HEAD_PAD - QK_NOPE - QK_ROPE)], axis=2)
    wuq = w_uq.astype(_BF).reshape(DEPTH, Q_LORA, MLA_HEADS, QK_NOPE + QK_ROPE)
    wuq = jnp.pad(wuq, ((0, 0), (0, 0), (0, 0), (0, HEAD_PAD - QK_NOPE - QK_ROPE)))
    wukv = w_ukv.astype(_BF).reshape(DEPTH, KV_LORA, MLA_HEADS, QK_NOPE + V_HEAD)
    wk = jnp.pad(wukv[..., :QK_NOPE], ((0, 0), (0, 0), (0, 0), (0, HEAD_PAD - QK_NOPE)))
    wv = wukv[..., QK_NOPE:].reshape(DEPTH, KV_LORA, MLA_HEADS // 2, 2, V_HEAD)
    zv = jnp.zeros_like(wv[..., 0, :])
    wv = jnp.stack([wv[..., 0, :], zv, zv, wv[..., 1, :]], axis=3)
    rows = lambda a: a.reshape(DEPTH, 1, -1)
    return {
        "w_in": w_in_bf,
        "wc": wc,
        "w3": w_in_bf[:, :, IN_BACK:],
        "conv_a_w": jnp.broadcast_to(
            conv_a_w.reshape(DEPTH, CONF_KERNEL, 1, N_SLAB, LANES).transpose(0, 3, 1, 2, 4),
            (DEPTH, N_SLAB, CONF_KERNEL, SUBLANES, LANES)),
        "conv_a_b": jnp.broadcast_to(
            conv_a_b.reshape(DEPTH, N_SLAB, 1, LANES), (DEPTH, N_SLAB, SUBLANES, LANES)),
        "ln_a_g": rows(ln_a_g), "ln_a_b": rows(ln_a_b), "conv_b_w": conv_b_w,
        "q_norm_g": rows(q_norm_g), "kv_norm_g": rows(kv_norm_g),
        "wuq": wuq.reshape(DEPTH, Q_LORA, QK_PAD),
        "wukv": jnp.concatenate(
            [wk.reshape(DEPTH, KV_LORA, QK_PAD), wv.reshape(DEPTH, KV_LORA, QK_PAD)], axis=2),
        "w_a_out": w_a_out.astype(_BF), "w_b_out": w_b_out.astype(_BF),
        "w_c_out": w_c_out.astype(_BF), "w_o": w_o.astype(_BF),
        "ln_g": rows(ln_g), "ln_b": rows(ln_b),
    }


def kernel(x, c, positions, w_ada, b_ada, w_in, conv_a_w, conv_a_b, ln_a_g, ln_a_b, w_a_out, conv_b_w, w_b_out, q_norm_g, kv_norm_g, w_uq, w_ukv, w_c_out, w_o, ln_g, ln_b):
    assert x.shape == (BATCH, SEQ, D_MODEL) and x.dtype == _F32
    assert w_in.shape == (DEPTH, D_MODEL, D_IN)
    ada = _ada_all_layers(c, w_ada, b_ada)
    c_tab, s_tab = _rope_tables(positions)
    pw = _prepared_weights(w_in, conv_a_w, conv_a_b, ln_a_g, ln_a_b, w_a_out, conv_b_w, w_b_out,
                           q_norm_g, kv_norm_g, w_uq, w_ukv, w_c_out, w_o, ln_g, ln_b)
    h = x.reshape(BATCH * SEQ, D_MODEL)
    for l in range(DEPTH):
        za, zb, q, k, v = _front(h, ada, pw, l, c_tab, s_tab)
        o = _attention(q, k, v)
        h = _back(h, ada, za, zb, o, pw, l)
    return h.reshape(BATCH, SEQ, D_MODEL)
```

```python
import jax
import jax.numpy as jnp
import numpy as np
from jax import lax
from jax.experimental import pallas as pl
from jax.experimental.pallas import tpu as pltpu

D_MODEL = 1024
BATCH = 8
SEQ = 2048
DEPTH = 4
CONF_WIDTH = 512
CONF_KERNEL = 31
SC_WIDTH = 512
SC_KERNEL = 3
MLA_HEADS = 8
QK_NOPE = 64
QK_ROPE = 32
V_HEAD = 64
Q_LORA = 384
KV_LORA = 256
MLA_WIDTH = MLA_HEADS * V_HEAD
ROPE_THETA = 10000.0
N_BRANCH = 3
LN_EPS = 1e-5
RMS_EPS = 1e-6
DEEPNORM_ALPHA = (2 * DEPTH) ** 0.25

LANES = 128
SUBLANES = 8
HEAD_PAD = LANES
HALF_ROPE = QK_ROPE // 2
QK_PAD = MLA_HEADS * HEAD_PAD
A_COLS = 3 * CONF_WIDTH
B_COLS = 4 * SC_WIDTH
C_COLS = Q_LORA + KV_LORA + HEAD_PAD
BACK_COLS = MLA_WIDTH + N_BRANCH * D_MODEL
KV_UP_COLS = 2 * QK_PAD
IN_Q = A_COLS + B_COLS
IN_KR = IN_Q + Q_LORA + KV_LORA
IN_BACK = IN_KR + QK_ROPE
D_IN = IN_BACK + BACK_COLS

ROW_TILE = 512
CONV_CHUNK = 32
N_SEG = SUBLANES
SEG = ROW_TILE // N_SEG
PITCH = SEG + 4
N_SLAB = CONF_WIDTH // LANES
N_BQ = B_COLS // SC_WIDTH
CONV_GROUP = 8
B_HALO = 8
ATT_TILE = 256
ATT_HEADS = 8
VMEM_LIMIT = 56 * 1024 * 1024

_BF = jnp.bfloat16
_F32 = jnp.float32


def _dot(a, b):
    return jnp.dot(a, b, preferred_element_type=_F32)


def _sigmoid(x):
    return jax.nn.sigmoid(x)


def _silu(x):
    return x * jax.nn.sigmoid(x)


def _layer_spec(shape, l, col_block=0):
    zeros = (0,) * (len(shape) - 1)
    return pl.BlockSpec((None,) + tuple(shape), lambda b, s: (l,) + zeros + (col_block,))


def _ada_kernel(c_ref, w_ref, b_ref, o_ref):
    c = c_ref[...]
    c_act = c * jax.nn.sigmoid(c)
    o_ref[...] = _dot(c_act, w_ref[...]) + b_ref[...]


def _ada_all_layers(c, w_ada, b_ada):
    b_r = b_ada.reshape(DEPTH, 3, 1, D_MODEL)
    ada = pl.pallas_call(
        _ada_kernel,
        grid=(DEPTH, 3),
        in_specs=[
            pl.BlockSpec((BATCH, D_MODEL), lambda l, j: (0, 0)),
            pl.BlockSpec((None, D_MODEL, D_MODEL), lambda l, j: (l, 0, j)),
            pl.BlockSpec((None, None, 1, D_MODEL), lambda l, j: (l, j, 0, 0)),
        ],
        out_specs=pl.BlockSpec((None, None, BATCH, D_MODEL), lambda l, j: (l, j, 0, 0)),
        out_shape=jax.ShapeDtypeStruct((DEPTH, 3, BATCH, D_MODEL), _F32),
        name="ada",
    )(c, w_ada, b_r)
    return ada.reshape(DEPTH, 3, BATCH, 1, D_MODEL)


def _ada_spec(l, which):
    return pl.BlockSpec((None, None, None, 1, D_MODEL), lambda b, s: (l, which, b, 0, 0))


def _angle_kernel(pos_ref, invf_ref, cos_ref, sin_ref):
    ang = pos_ref[...].astype(_F32) * invf_ref[...]
    cos_ref[...] = jnp.cos(ang)
    sin_ref[...] = jnp.sin(ang)


def _expand_kernel(cos_ref, sin_ref, ec_ref, es_ref, base_ref, c_out, s_out):
    hi = lax.Precision.HIGHEST
    c_out[...] = jnp.dot(cos_ref[...], ec_ref[...], precision=hi, preferred_element_type=_F32) + base_ref[...]
    s_out[...] = jnp.dot(sin_ref[...], es_ref[...], precision=hi, preferred_element_type=_F32)


def _rope_tables(positions):
    n_tok = BATCH * SEQ
    rows = n_tok * HALF_ROPE // LANES
    inv_freq = ROPE_THETA ** (-jnp.arange(0, QK_ROPE, 2, dtype=_F32) / QK_ROPE)
    pos_rep = jnp.repeat(positions.reshape(-1), HALF_ROPE).reshape(rows, LANES)
    invf_rep = jnp.tile(inv_freq, LANES // HALF_ROPE).reshape(1, LANES)
    cos_d, sin_d = pl.pallas_call(
        _angle_kernel,
        out_shape=[jax.ShapeDtypeStruct((rows, LANES), _F32)] * 2,
        name="rope_angles",
    )(pos_rep, invf_rep)
    cos16 = cos_d.reshape(n_tok, HALF_ROPE)
    sin16 = sin_d.reshape(n_tok, HALF_ROPE)

    eye = np.eye(HALF_ROPE, dtype=np.float32)
    ec = np.zeros((HALF_ROPE, LANES), np.float32)
    es = np.zeros((HALF_ROPE, LANES), np.float32)
    ec[:, QK_NOPE:QK_NOPE + HALF_ROPE] = eye
    ec[:, QK_NOPE + HALF_ROPE:QK_NOPE + QK_ROPE] = eye
    es[:, QK_NOPE:QK_NOPE + HALF_ROPE] = -eye
    es[:, QK_NOPE + HALF_ROPE:QK_NOPE + QK_ROPE] = eye
    base = np.ones((1, LANES), np.float32)
    base[:, QK_NOPE:QK_NOPE + QK_ROPE] = 0.0

    blk = 2048
    return pl.pallas_call(
        _expand_kernel,
        grid=(n_tok // blk,),
        in_specs=[
            pl.BlockSpec((blk, HALF_ROPE), lambda i: (i, 0)),
            pl.BlockSpec((blk, HALF_ROPE), lambda i: (i, 0)),
            pl.BlockSpec((HALF_ROPE, LANES), lambda i: (0, 0)),
            pl.BlockSpec((HALF_ROPE, LANES), lambda i: (0, 0)),
            pl.BlockSpec((1, LANES), lambda i: (0, 0)),
        ],
        out_specs=[pl.BlockSpec((blk, LANES), lambda i: (i, 0))] * 2,
        out_shape=[jax.ShapeDtypeStruct((n_tok, LANES), _F32)] * 2,
        name="rope_expand",
    )(cos16, sin16, jnp.asarray(ec), jnp.asarray(es), jnp.asarray(base))


def _rope128(t, c_tab, s_tab, lo_half):
    swapped = jnp.where(lo_half, pltpu.roll(t, LANES - HALF_ROPE, 1), pltpu.roll(t, HALF_ROPE, 1))
    return t * c_tab + swapped * s_tab


def _front_kernel(x_ref, shift_ref, scale_ref, wa_ref, wb0_ref, wb1_ref, wb2_ref, wb3_ref, wc_ref,
                  cw_ref, cb_ref, lag_ref, lab_ref, cbw_ref, qg_ref, kvg_ref,
                  wuq_ref, wukv_ref, vones_ref, ctab_ref, stab_ref,
                  za_ref, zb_ref, q_ref, k_ref, v_ref,
                  u_ref, pa_ref, pb_ref, abuf_ref, ybuf_ref, bbuf_ref,
                  qn_ref, kvn_ref, kr_ref, qup_ref, kvup_ref):
    tm = ROW_TILE

    @pl.when(pl.program_id(1) == 0)
    def _():
        abuf_ref[:, 0:SEG, :] = jnp.zeros((N_SLAB, SEG, LANES), _F32)
        bbuf_ref[0:B_HALO, :] = jnp.zeros((B_HALO, SC_WIDTH), _F32)

    u_ref[...] = (x_ref[...] * (1.0 + scale_ref[...]) + shift_ref[...]).astype(_BF)
    pa_ref[...] = _dot(u_ref[...], wa_ref[...])

    glu = pa_ref[:, 0:CONF_WIDTH] * _sigmoid(pa_ref[:, CONF_WIDTH:2 * CONF_WIDTH])
    for s in range(N_SEG):
        for c in range(N_SLAB):
            abuf_ref[c, PITCH * (s + 1):PITCH * (s + 1) + SEG, :] = (
                glu[SEG * s:SEG * (s + 1), c * LANES:(c + 1) * LANES])
    pc = _dot(u_ref[...], wc_ref[...])
    q_lat = pc[:, 0:Q_LORA]
    kv_lat = pc[:, Q_LORA:Q_LORA + KV_LORA]
    kr_ref[...] = pc[:, Q_LORA + KV_LORA:C_COLS]
    qn = q_lat * lax.rsqrt(jnp.mean(q_lat * q_lat, axis=-1, keepdims=True) + RMS_EPS) * qg_ref[...]
    kvn = kv_lat * lax.rsqrt(jnp.mean(kv_lat * kv_lat, axis=-1, keepdims=True) + RMS_EPS) * kvg_ref[...]
    qn_ref[...] = qn.astype(_BF)
    kvn_ref[...] = kvn.astype(_BF)
    wb_refs = (wb0_ref, wb1_ref, wb2_ref, wb3_ref)
    for c in range(N_SLAB):
        pb_ref[c] = _dot(u_ref[...], wb_refs[c][...])
        for j0 in range(0, SEG, CONV_GROUP):
            accs = [cb_ref[c]] * CONV_GROUP
            for k in range(CONF_KERNEL):
                w = cw_ref[c, k]
                for jj in range(CONV_GROUP):
                    i = j0 + jj - (CONF_KERNEL - 1) + k
                    start = PITCH + i if i >= 0 else PITCH - (PITCH - SEG) + i
                    accs[jj] = accs[jj] + w * abuf_ref[c, pl.ds(start, N_SEG, stride=PITCH), :]
            for jj in range(CONV_GROUP):
                ybuf_ref[c, pl.ds(j0 + jj, N_SEG, stride=PITCH), :] = accs[jj]
        abuf_ref[c, 0:SEG, :] = abuf_ref[c, PITCH * N_SEG:PITCH * N_SEG + SEG, :]
    qup_ref[...] = _dot(qn_ref[...], wuq_ref[...])
    kvup_ref[...] = _dot(kvn_ref[...], wukv_ref[...])
    for s in range(N_SEG):
        rows = slice(SEG * s, SEG * (s + 1))
        conv = jnp.concatenate(
            [ybuf_ref[c, PITCH * s:PITCH * s + SEG, :] for c in range(N_SLAB)], axis=1)
        mu = jnp.mean(conv, axis=-1, keepdims=True)
        cen = conv - mu
        var = jnp.mean(cen * cen, axis=-1, keepdims=True)
        a_n = cen * lax.rsqrt(var + LN_EPS) * lag_ref[...] + lab_ref[...]
        gate = pa_ref[rows, 2 * CONF_WIDTH:3 * CONF_WIDTH]
        za_ref[rows, :] = (_silu(a_n) * _silu(gate)).astype(_BF)

    bbuf_ref[B_HALO:B_HALO + tm, :] = pb_ref[0] * pb_ref[2]
    firstb = B_HALO - (SC_KERNEL - 1)
    for r in range(0, tm, CONV_CHUNK):
        acc = cbw_ref[0:1, :] * bbuf_ref[firstb + r:firstb + r + CONV_CHUNK, :]
        for k in range(1, SC_KERNEL):
            acc = acc + cbw_ref[k:k + 1, :] * bbuf_ref[firstb + k + r:firstb + k + r + CONV_CHUNK, :]
        yb = pb_ref[1, r:r + CONV_CHUNK, :] * acc
        zb_ref[r:r + CONV_CHUNK, :] = (yb * _silu(pb_ref[3, r:r + CONV_CHUNK, :])).astype(_BF)
    bbuf_ref[0:B_HALO, :] = bbuf_ref[tm:tm + B_HALO, :]

    c_tab = ctab_ref[...]
    s_tab = stab_ref[...]
    lane = lax.broadcasted_iota(jnp.int32, (tm, LANES), 1)
    lo_half = lane < QK_NOPE + HALF_ROPE
    sm_scale = (QK_NOPE + QK_ROPE) ** -0.5 * np.log2(np.e)
    k_pe = _rope128(kr_ref[...], c_tab, s_tab, lo_half)
    for h in range(MLA_HEADS):
        sl = slice(h * HEAD_PAD, (h + 1) * HEAD_PAD)
        q_ref[:, sl] = (_rope128(qup_ref[:, sl], c_tab, s_tab, lo_half) * sm_scale).astype(_BF)
        k_ref[:, sl] = (kvup_ref[:, sl] + k_pe).astype(_BF)
    v_ref[...] = (kvup_ref[:, QK_PAD:KV_UP_COLS] + vones_ref[...]).astype(_BF)


def _v_ones_row():
    pair = np.concatenate([np.zeros(V_HEAD), np.ones(2 * V_HEAD), np.zeros(V_HEAD)]).astype(np.float32)
    return jnp.asarray(np.tile(pair, MLA_HEADS // 2).reshape(1, QK_PAD))


def _front(x, ada, pw, l, c_tab, s_tab):
    tm = ROW_TILE
    n_s = SEQ // tm
    row = lambda b, s: (b * n_s + s, 0)
    in_specs = [
        pl.BlockSpec((tm, D_MODEL), row),
        _ada_spec(l, 0),
        _ada_spec(l, 1),
        _layer_spec((D_MODEL, A_COLS), l),
    ] + [
        _layer_spec((D_MODEL, SC_WIDTH), l, col_block=A_COLS // SC_WIDTH + part) for part in range(N_BQ)
    ] + [
        _layer_spec((D_MODEL, C_COLS), l),
        _layer_spec((N_SLAB, CONF_KERNEL, SUBLANES, LANES), l),
        _layer_spec((N_SLAB, SUBLANES, LANES), l),
        _layer_spec((1, CONF_WIDTH), l),
        _layer_spec((1, CONF_WIDTH), l),
        _layer_spec((SC_KERNEL, SC_WIDTH), l),
        _layer_spec((1, Q_LORA), l),
        _layer_spec((1, KV_LORA), l),
        _layer_spec((Q_LORA, QK_PAD), l),
        _layer_spec((KV_LORA, KV_UP_COLS), l),
        pl.BlockSpec((1, QK_PAD), lambda b, s: (0, 0)),
        pl.BlockSpec((tm, LANES), row),
        pl.BlockSpec((tm, LANES), row),
    ]
    n_tok = BATCH * SEQ
    out_widths = (CONF_WIDTH, SC_WIDTH, QK_PAD, QK_PAD, QK_PAD)
    return pl.pallas_call(
        _front_kernel,
        grid=(BATCH, n_s),
        in_specs=in_specs,
        out_specs=[pl.BlockSpec((tm, w), row) for w in out_widths],
        out_shape=[jax.ShapeDtypeStruct((n_tok, w), _BF) for w in out_widths],
        scratch_shapes=[
            pltpu.VMEM((tm, D_MODEL), _BF),
            pltpu.VMEM((tm, A_COLS), _F32),
            pltpu.VMEM((N_BQ, tm, SC_WIDTH), _F32),
            pltpu.VMEM((N_SLAB, -(-(N_SEG + 1) * PITCH // SUBLANES) * SUBLANES, LANES), _F32),
            pltpu.VMEM((N_SLAB, N_SEG * PITCH, LANES), _F32),
            pltpu.VMEM((tm + B_HALO, SC_WIDTH), _F32),
            pltpu.VMEM((tm, Q_LORA), _BF),
            pltpu.VMEM((tm, KV_LORA), _BF),
            pltpu.VMEM((tm, HEAD_PAD), _F32),
            pltpu.VMEM((tm, QK_PAD), _F32),
            pltpu.VMEM((tm, KV_UP_COLS), _F32),
        ],
        compiler_params=pltpu.CompilerParams(
            dimension_semantics=("arbitrary", "arbitrary"), vmem_limit_bytes=VMEM_LIMIT),
        name="front",
    )(x, ada, ada, pw["w_ab"], pw["w_ab"], pw["w_ab"], pw["w_ab"], pw["w_ab"], pw["wc"],
      pw["conv_a_w"], pw["conv_a_b"], pw["ln_a_g"], pw["ln_a_b"], pw["conv_b_w"],
      pw["q_norm_g"], pw["kv_norm_g"], pw["wuq"], pw["wukv"], _v_ones_row(), c_tab, s_tab)


def _attn_kernel(q_ref, k_ref, v_ref, o_ref, s_buf, mx_ref, acc_ref):
    t = ATT_TILE
    qi = pl.program_id(2)
    nt = (((1,), (1,)), ((), ()))
    heads = range(ATT_HEADS)
    hs = [slice(h * HEAD_PAD, (h + 1) * HEAD_PAD) for h in heads]

    def scores(j, h):
        kb = k_ref[pl.ds(pl.multiple_of(j * t, t), t), hs[h]]
        return lax.dot_general(q_ref[:, hs[h]], kb, nt, preferred_element_type=_F32)

    def lane_max(s):
        out = s[:, 0:LANES]
        for c in range(1, t // LANES):
            out = jnp.maximum(out, s[:, c * LANES:(c + 1) * LANES])
        return out

    for h in heads:
        mx_ref[h] = jnp.full((t, LANES), -jnp.inf, _F32)
        acc_ref[h] = jnp.zeros((t, LANES), _F32)

    def pass1(j0, n_blocks):
        for h in heads:
            mx = mx_ref[h]
            for b in range(n_blocks):
                s = scores(j0 + b, h)
                s_buf[j0 + b, h] = s
                mx = jnp.maximum(mx, lane_max(s))
            mx_ref[h] = mx

    def pass1_pair(jj, carry):
        pass1(2 * jj, 2)
        return carry

    lax.fori_loop(0, qi // 2, pass1_pair, 0)

    @pl.when(qi % 2 == 1)
    def _():
        pass1(qi - 1, 1)

    row = lax.broadcasted_iota(jnp.int32, (t, t), 0)
    col = lax.broadcasted_iota(jnp.int32, (t, t), 1)
    causal = col <= row
    for h in heads:
        s = jnp.where(causal, scores(qi, h), -jnp.inf)
        s_buf[qi, h] = s
        mx_ref[h] = jnp.maximum(mx_ref[h], lane_max(s))

    m = [jnp.max(mx_ref[h], axis=-1, keepdims=True) for h in heads]

    def pass2(j0, n_blocks):
        vbs = [v_ref[pl.ds(pl.multiple_of((j0 + b) * t, t), t), :] for b in range(n_blocks)]
        for h in heads:
            acc = acc_ref[h]
            for b in range(n_blocks):
                p = jnp.exp2(s_buf[j0 + b, h] - m[h]).astype(_BF)
                acc = acc + _dot(p, vbs[b][:, hs[h]])
            acc_ref[h] = acc

    def pass2_pair(jj, carry):
        pass2(2 * jj, 2)
        return carry

    n_keys = qi + 1
    lax.fori_loop(0, n_keys // 2, pass2_pair, 0)

    @pl.when(n_keys % 2 == 1)
    def _():
        pass2(qi, 1)

    lane = lax.broadcasted_iota(jnp.int32, (t, LANES), 1)
    for g in range(ATT_HEADS // 2):
        even = acc_ref[2 * g]
        odd = acc_ref[2 * g + 1]
        den = jnp.where(lane < V_HEAD, pltpu.roll(even, V_HEAD, 1), pltpu.roll(odd, V_HEAD, 1))
        num = jnp.where(lane < V_HEAD, even, odd)
        o_ref[:, g * LANES:(g + 1) * LANES] = (num / den).astype(o_ref.dtype)


def _attention(q, k, v):
    t = ATT_TILE
    n_q = SEQ // t
    n_grp = MLA_HEADS // ATT_HEADS
    q3 = q.reshape(BATCH, SEQ, QK_PAD)
    k3 = k.reshape(BATCH, SEQ, QK_PAD)
    v3 = v.reshape(BATCH, SEQ, QK_PAD)
    o = pl.pallas_call(
        _attn_kernel,
        grid=(BATCH, n_grp, n_q),
        in_specs=[
            pl.BlockSpec((None, t, ATT_HEADS * HEAD_PAD), lambda b, g, i: (b, i, g)),
            pl.BlockSpec((None, SEQ, ATT_HEADS * HEAD_PAD), lambda b, g, i: (b, 0, g)),
            pl.BlockSpec((None, SEQ, ATT_HEADS * HEAD_PAD), lambda b, g, i: (b, 0, g)),
        ],
        out_specs=pl.BlockSpec((None, t, ATT_HEADS * V_HEAD), lambda b, g, i: (b, i, g)),
        out_shape=jax.ShapeDtypeStruct((BATCH, SEQ, MLA_WIDTH), _BF),
        scratch_shapes=[
            pltpu.VMEM((n_q, ATT_HEADS, t, t), _F32),
            pltpu.VMEM((ATT_HEADS, t, LANES), _F32),
            pltpu.VMEM((ATT_HEADS, t, LANES), _F32),
        ],
        compiler_params=pltpu.CompilerParams(
            dimension_semantics=("arbitrary", "arbitrary", "arbitrary"), vmem_limit_bytes=VMEM_LIMIT),
        name="attn",
    )(q3, k3, v3)
    return o.reshape(BATCH * SEQ, MLA_WIDTH)


def _back_kernel(x_ref, shift_ref, scale_ref, gate_ref, za_ref, zb_ref, o_ref,
                 w3_ref, wao_ref, wbo_ref, wco_ref, wo_ref, lg_ref, lb_ref, out_ref):
    x = x_ref[...]
    u = (x * (1.0 + scale_ref[...]) + shift_ref[...]).astype(_BF)
    p3 = _dot(u, w3_ref[...])
    c_gate = p3[:, 0:MLA_WIDTH]
    oc = (o_ref[...].astype(_F32) * _silu(c_gate)).astype(_BF)
    m = _sigmoid(p3[:, MLA_WIDTH:MLA_WIDTH + D_MODEL]) * _dot(za_ref[...], wao_ref[...])
    m = m + _sigmoid(p3[:, MLA_WIDTH + D_MODEL:MLA_WIDTH + 2 * D_MODEL]) * _dot(zb_ref[...], wbo_ref[...])
    m = m + _sigmoid(p3[:, MLA_WIDTH + 2 * D_MODEL:BACK_COLS]) * _dot(oc, wco_ref[...])
    out = _dot(m.astype(_BF), wo_ref[...])
    r = DEEPNORM_ALPHA * x + gate_ref[...] * out
    mu = jnp.mean(r, axis=-1, keepdims=True)
    cen = r - mu
    var = jnp.mean(cen * cen, axis=-1, keepdims=True)
    out_ref[...] = cen * lax.rsqrt(var + LN_EPS) * lg_ref[...] + lb_ref[...]


def _back(x, ada, za, zb, o, pw, l):
    tm = ROW_TILE
    n_s = SEQ // tm
    row = lambda b, s: (b * n_s + s, 0)
    in_specs = [
        pl.BlockSpec((tm, D_MODEL), row),
        _ada_spec(l, 0),
        _ada_spec(l, 1),
        _ada_spec(l, 2),
        pl.BlockSpec((tm, CONF_WIDTH), row),
        pl.BlockSpec((tm, SC_WIDTH), row),
        pl.BlockSpec((tm, MLA_WIDTH), row),
        _layer_spec((D_MODEL, BACK_COLS), l),
        _layer_spec((CONF_WIDTH, D_MODEL), l),
        _layer_spec((SC_WIDTH, D_MODEL), l),
        _layer_spec((MLA_WIDTH, D_MODEL), l),
        _layer_spec((D_MODEL, D_MODEL), l),
        _layer_spec((1, D_MODEL), l),
        _layer_spec((1, D_MODEL), l),
    ]
    return pl.pallas_call(
        _back_kernel,
        grid=(BATCH, n_s),
        in_specs=in_specs,
        out_specs=pl.BlockSpec((tm, D_MODEL), row),
        out_shape=jax.ShapeDtypeStruct((BATCH * SEQ, D_MODEL), _F32),
        compiler_params=pltpu.CompilerParams(
            dimension_semantics=("arbitrary", "arbitrary"), vmem_limit_bytes=VMEM_LIMIT),
        name="back",
    )(x, ada, ada, ada, za, zb, o, pw["w3"], pw["w_a_out"], pw["w_b_out"], pw["w_c_out"],
      pw["w_o"], pw["ln_g"], pw["ln_b"])


def _prepared_weights(w_in, conv_a_w, conv_a_b, ln_a_g, ln_a_b, w_a_out, conv_b_w, w_b_out,
                      q_norm_g, kv_norm_g, w_uq, w_ukv, w_c_out, w_o, ln_g, ln_b):
    zeros = lambda n: jnp.zeros((DEPTH, D_MODEL, n), _BF)
    wc = jnp.concatenate(
        [w_in[:, :, IN_Q:IN_KR].astype(_BF), zeros(QK_NOPE), w_in[:, :, IN_KR:IN_BACK].astype(_BF),
         zeros(HEAD_PAD - QK_NOPE - QK_ROPE)], axis=2)
    wuq = w_uq.astype(_BF).reshape(DEPTH, Q_LORA, MLA_HEADS, QK_NOPE + QK_ROPE)
    wuq = jnp.pad(wuq, ((0, 0), (0, 0), (0, 0), (0, HEAD_PAD - QK_NOPE - QK_ROPE)))
    wukv = w_ukv.astype(_BF).reshape(DEPTH, KV_LORA, MLA_HEADS, QK_NOPE + V_HEAD)
    wk = jnp.pad(wukv[..., :QK_NOPE], ((0, 0), (0, 0), (0, 0), (0, HEAD_PAD - QK_NOPE)))
    wv = wukv[..., QK_NOPE:].reshape(DEPTH, KV_LORA, MLA_HEADS // 2, 2, V_HEAD)
    zv = jnp.zeros_like(wv[..., 0, :])
    wv = jnp.stack([wv[..., 0, :], zv, zv, wv[..., 1, :]], axis=3)
    rows = lambda a: a.reshape(DEPTH, 1, -1)
    return {
        "w_ab": w_in[:, :, :IN_Q].astype(_BF),
        "wc": wc,
        "w3": w_in[:, :, IN_BACK:].astype(_BF),
        "conv_a_w": jnp.broadcast_to(
            conv_a_w.reshape(DEPTH, CONF_KERNEL, 1, N_SLAB, LANES).transpose(0, 3, 1, 2, 4),
            (DEPTH, N_SLAB, CONF_KERNEL, SUBLANES, LANES)),
        "conv_a_b": jnp.broadcast_to(
            conv_a_b.reshape(DEPTH, N_SLAB, 1, LANES), (DEPTH, N_SLAB, SUBLANES, LANES)),
        "ln_a_g": rows(ln_a_g), "ln_a_b": rows(ln_a_b), "conv_b_w": conv_b_w,
        "q_norm_g": rows(q_norm_g), "kv_norm_g": rows(kv_norm_g),
        "wuq": wuq.reshape(DEPTH, Q_LORA, QK_PAD),
        "wukv": jnp.concatenate(
            [wk.reshape(DEPTH, KV_LORA, QK_PAD), wv.reshape(DEPTH, KV_LORA, QK_PAD)], axis=2),
        "w_a_out": w_a_out.astype(_BF), "w_b_out": w_b_out.astype(_BF),
        "w_c_out": w_c_out.astype(_BF), "w_o": w_o.astype(_BF),
        "ln_g": rows(ln_g), "ln_b": rows(ln_b),
    }


def kernel(x, c, positions, w_ada, b_ada, w_in, conv_a_w, conv_a_b, ln_a_g, ln_a_b, w_a_out, conv_b_w, w_b_out, q_norm_g, kv_norm_g, w_uq, w_ukv, w_c_out, w_o, ln_g, ln_b):
    assert x.shape == (BATCH, SEQ, D_MODEL) and x.dtype == _F32
    assert w_in.shape == (DEPTH, D_MODEL, D_IN)
    ada = _ada_all_layers(c, w_ada, b_ada)
    c_tab, s_tab = _rope_tables(positions)
    pw = _prepared_weights(w_in, conv_a_w, conv_a_b, ln_a_g, ln_a_b, w_a_out, conv_b_w, w_b_out,
                           q_norm_g, kv_norm_g, w_uq, w_ukv, w_c_out, w_o, ln_g, ln_b)
    h = x.reshape(BATCH * SEQ, D_MODEL)
    for l in range(DEPTH):
        za, zb, q, k, v = _front(h, ada, pw, l, c_tab, s_tab)
        o = _attention(q, k, v)
        h = _back(h, ada, za, zb, o, pw, l)
    return h.reshape(BATCH, SEQ, D_MODEL)
```
